```python
import math
import jax, jax.numpy as jnp
from jax import lax
import numpy as np

D_MODEL = 1024
BATCH = 16
SEQ = 2048
DEPTH = 2
DEC_BATCH = 32
DEC_SEQ = 64
PAST_LEN = 2048

CHUNK = 64
PREV_CHUNKS = 8
BAND_PREV = PREV_CHUNKS * CHUNK
BAND = BAND_PREV + CHUNK
MAX_REL = 128
N_REL = 2 * MAX_REL + 1
HA = 8
DA = 64
HB = 4
DB = 64
EB = 2 * DB
GROUP_W = HA * DA
MIX_W = 2 * GROUP_W
D_IN = 6 * GROUP_W
D_FF = 2816
Q_BLOCK = 128
ROPE_THETA = 10000.0
EPS = 1e-6
NEG = -1e30

kernel_name = "hybrid_chunkband_diffattn_macaron_step"


def rms_norm(x, g):
    xf = x.astype(jnp.float32)
    y = xf * lax.rsqrt(jnp.mean(xf * xf, axis=-1, keepdims=True) + EPS)
    return (y * g.astype(jnp.float32)).astype(x.dtype)


def rope(x, pos):
    d = x.shape[-1]
    half = d // 2
    inv = ROPE_THETA ** (-jnp.arange(half, dtype=jnp.float32) * 2.0 / d)
    ang = pos.astype(jnp.float32)[:, None] * inv[None, :]
    cos = jnp.cos(ang)[:, None, :]
    sin = jnp.sin(ang)[:, None, :]
    xf = x.astype(jnp.float32)
    x1, x2 = xf[..., :half], xf[..., half:]
    return jnp.concatenate([x1 * cos - x2 * sin, x2 * cos + x1 * sin], axis=-1).astype(x.dtype)


def ffn_half(x, g, w_gate, w_up, w_down):
    h = rms_norm(x, g)
    a = jax.nn.silu(jnp.einsum('bsd,df->bsf', h, w_gate)) * jnp.einsum('bsd,df->bsf', h, w_up)
    return x + 0.5 * jnp.einsum('bsf,fd->bsd', a, w_down)


def project(h, w_in, g_qa, g_ka, g_qb, g_kb, pos):
    b, s, _ = h.shape
    z = jnp.einsum('bsd,de->bse', h, w_in)
    qa, ka, va, qb, kb, vb = jnp.split(z, 6, axis=-1)
    qa = rms_norm(qa.reshape(b, s, HA, DA), g_qa)
    ka = rms_norm(ka.reshape(b, s, HA, DA), g_ka)
    va = va.reshape(b, s, HA, DA)
    qb = rope(rms_norm(qb.reshape(b, s, 2 * HB, DB), g_qb), pos).reshape(b, s, HB, 2, DB)
    kb = rope(rms_norm(kb.reshape(b, s, 2 * HB, DB), g_kb), pos).reshape(b, s, HB, 2, DB)
    vb = vb.reshape(b, s, HB, EB)
    return qa, ka, va, qb, kb, vb


def rel_bias_lookup(table, qpos, kpos):
    idx = jnp.clip(qpos[:, None] - kpos[None, :], -MAX_REL, MAX_REL) + MAX_REL
    return table[:, idx]


def band_core(q, k, v, bias, valid):
    s = jnp.einsum('bqhd,bkhd->bhqk', q, k).astype(jnp.float32) * (DA ** -0.5)
    s = s + bias[None].astype(jnp.float32)
    s = jnp.where(valid[None, None, None, :], s, NEG)
    p = jax.nn.softmax(s, axis=-1).astype(v.dtype)
    return jnp.einsum('bhqk,bkhd->bqhd', p, v)


def band_attn_prompt(q, k, v, bias):
    b, s, h, d = q.shape
    nc = s // CHUNK
    pad = ((0, 0), (BAND_PREV, 0), (0, 0), (0, 0))
    kp = jnp.pad(k, pad)
    vp = jnp.pad(v, pad)
    qc = jnp.moveaxis(q.reshape(b, nc, CHUNK, h, d), 1, 0)
    rows = jnp.arange(BAND)

    def one_chunk(args):
        c, qch = args
        start = c * CHUNK
        kband = lax.dynamic_slice_in_dim(kp, start, BAND, axis=1)
        vband = lax.dynamic_slice_in_dim(vp, start, BAND, axis=1)
        valid = rows >= BAND_PREV - start
        return band_core(qch, kband, vband, bias, valid)

    out = lax.map(one_chunk, (jnp.arange(nc), qc))
    return jnp.moveaxis(out, 0, 1).reshape(b, s, h * d)


def diff_core(q, k, v, mask, lam):
    s = jnp.einsum('bqhmd,bkhmd->bhmqk', q, k).astype(jnp.float32) * (DB ** -0.5)
    s = jnp.where(mask[None, None, None], s, NEG)
    p = jax.nn.softmax(s, axis=-1)
    a = p[:, :, 0] - lam * p[:, :, 1]
    return jnp.einsum('bhqk,bkhe->bqhe', a.astype(v.dtype), v)


def diff_attn_prompt(q, k, v, lam):
    b, s = q.shape[0], q.shape[1]
    nb = s // Q_BLOCK
    qblk = jnp.moveaxis(q.reshape(b, nb, Q_BLOCK, HB, 2, DB), 1, 0)
    kchunk = jnp.arange(s) // CHUNK

    def one_block(args):
        i, qb = args
        qchunk = (i * Q_BLOCK + jnp.arange(Q_BLOCK)) // CHUNK
        mask = kchunk[None, :] <= qchunk[:, None]
        return diff_core(qb, k, v, mask, lam)

    out = lax.map(one_block, (jnp.arange(nb), qblk))
    return jnp.moveaxis(out, 0, 1).reshape(b, s, HB, EB)


def diff_post(o, g_sub, lam_init):
    b, s = o.shape[0], o.shape[1]
    return (rms_norm(o, g_sub) * (1.0 - lam_init)).reshape(b, s, HB * EB)


def out_proj(oa, ob, w_out):
    return jnp.einsum('bse,ed->bsd', jnp.concatenate([oa, ob], axis=-1), w_out)


def setup_inputs(seed: int = 0) -> dict:
    key = jax.random.key(seed)
    ks = jax.random.split(key, 32)
    f = jnp.float32
    nrm = lambda k, shape, scale: jax.random.normal(k, shape, f) * scale
    gain = lambda k, shape: 1.0 + 0.02 * jax.random.normal(k, shape, f)
    a_len = min(BAND_PREV, PAST_LEN)
    return {
        'x_prompt': nrm(ks[0], (BATCH, SEQ, D_MODEL), 1.0),
        'x_sample': nrm(ks[1], (DEC_BATCH, DEC_SEQ, D_MODEL), 1.0),
        'cache_a_k': nrm(ks[2], (DEPTH, DEC_BATCH, a_len, HA, DA), 1.0),
        'cache_a_v': nrm(ks[3], (DEPTH, DEC_BATCH, a_len, HA, DA), 1.0),
        'cache_b_k': nrm(ks[4], (DEPTH, DEC_BATCH, PAST_LEN, HB, EB), 1.0),
        'cache_b_v': nrm(ks[5], (DEPTH, DEC_BATCH, PAST_LEN, HB, EB), 1.0),
        'g_ffn1': gain(ks[6], (DEPTH, D_MODEL)),
        'w1_gate': nrm(ks[7], (DEPTH, D_MODEL, D_FF), D_MODEL ** -0.5),
        'w1_up': nrm(ks[8], (DEPTH, D_MODEL, D_FF), D_MODEL ** -0.5),
        'w1_down': nrm(ks[9], (DEPTH, D_FF, D_MODEL), D_FF ** -0.5),
        'g_mix': gain(ks[10], (DEPTH, D_MODEL)),
        'w_in': nrm(ks[11], (DEPTH, D_MODEL, D_IN), D_MODEL ** -0.5),
        'g_qa': gain(ks[12], (DEPTH, DA)),
        'g_ka': gain(ks[13], (DEPTH, DA)),
        'g_qb': gain(ks[14], (DEPTH, DB)),
        'g_kb': gain(ks[15], (DEPTH, DB)),
        'rel_bias': nrm(ks[16], (DEPTH, HA, N_REL), 0.1),
        'lam_q1': nrm(ks[17], (DEPTH, DB), 0.1),
        'lam_k1': nrm(ks[18], (DEPTH, DB), 0.1),
        'lam_q2': nrm(ks[19], (DEPTH, DB), 0.1),
        'lam_k2': nrm(ks[20], (DEPTH, DB), 0.1),
        'g_sub': gain(ks[21], (DEPTH, EB)),
        'w_out': nrm(ks[22], (DEPTH, MIX_W, D_MODEL), MIX_W ** -0.5),
        'g_ffn2': gain(ks[23], (DEPTH, D_MODEL)),
        'w2_gate': nrm(ks[24], (DEPTH, D_MODEL, D_FF), D_MODEL ** -0.5),
        'w2_up': nrm(ks[25], (DEPTH, D_MODEL, D_FF), D_MODEL ** -0.5),
        'w2_down': nrm(ks[26], (DEPTH, D_FF, D_MODEL), D_FF ** -0.5),
    }


def reference(x_prompt, x_sample, cache_a_k, cache_a_v, cache_b_k, cache_b_v,
              g_ffn1, w1_gate, w1_up, w1_down, g_mix, w_in, g_qa, g_ka, g_qb, g_kb,
              rel_bias, lam_q1, lam_k1, lam_q2, lam_k2, g_sub, w_out,
              g_ffn2, w2_gate, w2_up, w2_down):
    bp, s = x_prompt.shape[0], x_prompt.shape[1]
    bd, t = x_sample.shape[0], x_sample.shape[1]
    past = cache_b_k.shape[2]
    a_len = cache_a_k.shape[2]
    keep_p = min(BAND_PREV, s)

    pos_p = jnp.arange(s)
    pos_s = past + jnp.arange(t)
    band_q = BAND_PREV + jnp.arange(CHUNK)
    band_k = jnp.arange(BAND)
    kpos_a = jnp.concatenate([past - a_len + jnp.arange(a_len), pos_s])
    valid_s = jnp.ones((a_len + t,), dtype=bool)
    mask_s = jnp.ones((t, past + t), dtype=bool)

    yp, ys = x_prompt, x_sample
    pak, pav, pbk, pbv, sak, sav, sbk, sbv = [], [], [], [], [], [], [], []
    for l in range(DEPTH):
        lam_init = 0.8 - 0.6 * math.exp(-0.3 * l)
        lam = (jnp.exp(jnp.sum(lam_q1[l].astype(jnp.float32) * lam_k1[l].astype(jnp.float32)))
               - jnp.exp(jnp.sum(lam_q2[l].astype(jnp.float32) * lam_k2[l].astype(jnp.float32)))
               + lam_init)

        yp = ffn_half(yp, g_ffn1[l], w1_gate[l], w1_up[l], w1_down[l])
        qa, ka, va, qb, kb, vb = project(rms_norm(yp, g_mix[l]), w_in[l], g_qa[l], g_ka[l], g_qb[l], g_kb[l], pos_p)
        oa = band_attn_prompt(qa, ka, va, rel_bias_lookup(rel_bias[l], band_q, band_k))
        ob = diff_post(diff_attn_prompt(qb, kb, vb, lam), g_sub[l], lam_init)
        yp = yp + out_proj(oa, ob, w_out[l])
        yp = ffn_half(yp, g_ffn2[l], w2_gate[l], w2_up[l], w2_down[l])
        pak.append(ka[:, s - keep_p:])
        pav.append(va[:, s - keep_p:])
        pbk.append(kb.reshape(bp, s, HB, EB))
        pbv.append(vb)

        ys = ffn_half(ys, g_ffn1[l], w1_gate[l], w1_up[l], w1_down[l])
        qa, ka, va, qb, kb, vb = project(rms_norm(ys, g_mix[l]), w_in[l], g_qa[l], g_ka[l], g_qb[l], g_kb[l], pos_s)
        ka_all = jnp.concatenate([cache_a_k[l], ka], axis=1)
        va_all = jnp.concatenate([cache_a_v[l], va], axis=1)
        oa = band_core(qa, ka_all, va_all, rel_bias_lookup(rel_bias[l], pos_s, kpos_a), valid_s).reshape(bd, t, GROUP_W)
        kb_all = jnp.concatenate([cache_b_k[l].reshape(bd, past, HB, 2, DB), kb], axis=1)
        vb_all = jnp.concatenate([cache_b_v[l], vb], axis=1)
        ob = diff_post(diff_core(qb, kb_all, vb_all, mask_s, lam), g_sub[l], lam_init)
        ys = ys + out_proj(oa, ob, w_out[l])
        ys = ffn_half(ys, g_ffn2[l], w2_gate[l], w2_up[l], w2_down[l])
        sak.append(ka)
        sav.append(va)
        sbk.append(kb.reshape(bd, t, HB, EB))
        sbv.append(vb)

    return (yp, ys,
            jnp.stack(pak), jnp.stack(pav), jnp.stack(pbk), jnp.stack(pbv),
            jnp.stack(sak), jnp.stack(sav), jnp.stack(sbk), jnp.stack(sbv))
```

```python
import functools
import math

import jax
import jax.numpy as jnp
from jax import lax
from jax.experimental import pallas as pl
from jax.experimental.pallas import tpu as pltpu

CHUNK = 64
PREV_CHUNKS = 8
BAND_PREV = PREV_CHUNKS * CHUNK
MAX_REL = 128
HA = 8
DA = 64
HB = 4
DB = 64
EB = 2 * DB
GROUP_W = HA * DA
ROPE_THETA = 10000.0
EPS = 1e-6
NEG = -1e30

LANES = 128
MXU_DIM = 256
TOKEN_TILE = 512
Q_TILE = 256
A_HALF = Q_TILE // 2
A_WIN = BAND_PREV + A_HALF
VMEM_LIMIT = 56 * 1024 * 1024

F32 = jnp.float32
BF16 = jnp.bfloat16


def _cparams(*sem):
    return pltpu.CompilerParams(dimension_semantics=sem, vmem_limit_bytes=VMEM_LIMIT)


def _resident(shape):
    nd = len(shape)
    return pl.BlockSpec(shape, lambda *_: (0,) * nd, pipeline_mode=pl.Buffered(1))


def _dot(a, b):
    return jnp.dot(a, b, preferred_element_type=F32)


def _dot_nt(a, b):
    return lax.dot_general(a, b, (((1,), (1,)), ((), ())), preferred_element_type=F32)


def _rms(x, g):
    return x * lax.rsqrt(jnp.mean(x * x, axis=-1, keepdims=True) + EPS) * g


def _swiglu_half(x, g, wg_ref, wu_ref, wd_ref, f_chunk):
    h = _rms(x, g).astype(BF16)
    d_ff = wg_ref.shape[1]
    y = None
    for c in range(0, d_ff, f_chunk):
        gate = _dot(h, wg_ref[:, c:c + f_chunk])
        up = _dot(h, wu_ref[:, c:c + f_chunk])
        a = (gate / (1.0 + jnp.exp(-gate)) * up).astype(BF16)
        part = _dot(a, wd_ref[c:c + f_chunk, :])
        y = part if y is None else y + part
    return x + 0.5 * y


def _ffn_kernel(x_ref, g_ref, wg_ref, wu_ref, wd_ref, o_ref, *, f_chunk):
    o_ref[...] = _swiglu_half(x_ref[...], g_ref[...], wg_ref, wu_ref, wd_ref, f_chunk)


def _out_ffn_kernel(x_ref, oa_ref, ob_ref, woa_ref, wob_ref, g_ref, wg_ref, wu_ref, wd_ref,
                    o_ref, *, f_chunk):
    x = x_ref[...] + _dot(oa_ref[...], woa_ref[...]) + _dot(ob_ref[...], wob_ref[...])
    o_ref[...] = _swiglu_half(x, g_ref[...], wg_ref, wu_ref, wd_ref, f_chunk)


def _f_chunk(d_ff):
    return d_ff // 2 if d_ff % (2 * LANES) == 0 else d_ff


def _ffn(x, g, wg, wu, wd):
    n, d = x.shape
    d_ff = wg.shape[1]
    assert n % TOKEN_TILE == 0
    row = pl.BlockSpec((TOKEN_TILE, d), lambda i: (i, 0))
    return pl.pallas_call(
        functools.partial(_ffn_kernel, f_chunk=_f_chunk(d_ff)),
        grid=(n // TOKEN_TILE,),
        in_specs=[row, _resident((1, d)), _resident((d, d_ff)), _resident((d, d_ff)),
                  _resident((d_ff, d))],
        out_specs=row,
        out_shape=jax.ShapeDtypeStruct((n, d), F32),
        compiler_params=_cparams("parallel"),
        name="ffn",
    )(x, g, wg, wu, wd)


def _out_ffn(x, oa, ob, woa, wob, g, wg, wu, wd):
    n, d = x.shape
    d_ff = wg.shape[1]
    row = pl.BlockSpec((TOKEN_TILE, d), lambda i: (i, 0))
    grp = pl.BlockSpec((TOKEN_TILE, GROUP_W), lambda i: (i, 0))
    return pl.pallas_call(
        functools.partial(_out_ffn_kernel, f_chunk=_f_chunk(d_ff)),
        grid=(n // TOKEN_TILE,),
        in_specs=[row, grp, grp, _resident((GROUP_W, d)), _resident((GROUP_W, d)),
                  _resident((1, d)), _resident((d, d_ff)), _resident((d, d_ff)),
                  _resident((d_ff, d))],
        out_specs=row,
        out_shape=jax.ShapeDtypeStruct((n, d), F32),
        compiler_params=_cparams("parallel"),
        name="out_ffn",
    )(x, oa, ob, woa, wob, g, wg, wu, wd)


def _head_norm(z, gain, ones_bd):
    outs = []
    for c in range(0, z.shape[1], MXU_DIM):
        zc = z[:, c:c + MXU_DIM]
        sq = zc * zc
        hi = sq.astype(BF16)
        lo = (sq - hi.astype(F32)).astype(BF16)
        ss = _dot(hi, ones_bd) + _dot(lo, ones_bd)
        outs.append(zc * lax.rsqrt(ss * (1.0 / DA) + EPS) * gain[:, c:c + MXU_DIM])
    return jnp.concatenate(outs, axis=1)


def _rope(y, cos2, sin2):
    first_half = (lax.broadcasted_iota(jnp.int32, (1, LANES), 1) % DB) < (DB // 2)
    outs = []
    for c in range(0, y.shape[1], LANES):
        yc = y[:, c:c + LANES]
        swapped = jnp.where(first_half, pltpu.roll(yc, LANES - DB // 2, 1),
                            pltpu.roll(yc, DB // 2, 1))
        outs.append(yc * cos2 + swapped * sin2)
    return jnp.concatenate(outs, axis=1)


def _proj_kernel(x_ref, g_ref, w_ref, gqa_ref, gka_ref, gqb_ref, gkb_ref, cos_ref, sin_ref,
                 qa_o, ka_o, va_o, qb_o, kb_o, vb_o, ka32_o, va32_o, kb32_o, vb32_o,
                 *, tiles_per_seq):
    h = _rms(x_ref[...], g_ref[...]).astype(BF16)
    r = lax.broadcasted_iota(jnp.int32, (MXU_DIM, MXU_DIM), 0) // DA
    c = lax.broadcasted_iota(jnp.int32, (MXU_DIM, MXU_DIM), 1) // DA
    ones_bd = (r == c).astype(BF16)
    cos2 = cos_ref[...]
    sin2 = sin_ref[...]

    def group(j):
        return _dot(h, w_ref[:, j * GROUP_W:(j + 1) * GROUP_W])

    qa_o[...] = _head_norm(group(0), gqa_ref[...], ones_bd).astype(BF16)
    ka = _head_norm(group(1), gka_ref[...], ones_bd)
    va = group(2)
    ka_o[...] = ka.astype(BF16)
    va_o[...] = va.astype(BF16)
    qb_o[...] = _rope(_head_norm(group(3), gqb_ref[...], ones_bd), cos2, sin2).astype(BF16)
    kb = _rope(_head_norm(group(4), gkb_ref[...], ones_bd), cos2, sin2)
    vb = group(5)
    kb_o[...] = kb.astype(BF16)
    vb_o[...] = vb.astype(BF16)
    kb32_o[...] = kb
    vb32_o[...] = vb

    @pl.when(pl.program_id(0) % tiles_per_seq == tiles_per_seq - 1)
    def _():
        ka32_o[...] = ka
        va32_o[...] = va


def _proj(x, g, w_in, gqa, gka, gqb, gkb, cos2, sin2, seq):
    n, d = x.shape
    keep = min(BAND_PREV, seq)
    if seq >= TOKEN_TILE:
        assert seq % TOKEN_TILE == 0 and keep == TOKEN_TILE
        tiles_per_seq = seq // TOKEN_TILE
    else:
        assert TOKEN_TILE % seq == 0 and keep == seq
        tiles_per_seq = 1
    n_keep = n // seq * keep
    pos_tiles = cos2.shape[0] // TOKEN_TILE
    row = pl.BlockSpec((TOKEN_TILE, d), lambda i: (i, 0))
    grp = pl.BlockSpec((TOKEN_TILE, GROUP_W), lambda i: (i, 0))
    kept = pl.BlockSpec((TOKEN_TILE, GROUP_W), lambda i: (i // tiles_per_seq, 0))
    tab = pl.BlockSpec((TOKEN_TILE, LANES), lambda i: (i % pos_tiles, 0))
    gain = _resident((1, GROUP_W))
    bf = jax.ShapeDtypeStruct((n, GROUP_W), BF16)
    return pl.pallas_call(
        functools.partial(_proj_kernel, tiles_per_seq=tiles_per_seq),
        grid=(n // TOKEN_TILE,),
        in_specs=[row, _resident((1, d)), _resident(w_in.shape), gain, gain, gain, gain, tab, tab],
        out_specs=[grp] * 6 + [kept, kept, grp, grp],
        out_shape=[bf] * 6 + [jax.ShapeDtypeStruct((n_keep, GROUP_W), F32)] * 2
                  + [jax.ShapeDtypeStruct((n, GROUP_W), F32)] * 2,
        compiler_params=_cparams("arbitrary"),
        name="proj",
    )(x, g, w_in, gqa, gka, gqb, gkb, cos2, sin2)


def _pair_masks(scale):
    lane = lax.broadcasted_iota(jnp.int32, (1, LANES), 1)
    lo = jnp.where(lane < DA, scale, 0.0).astype(BF16)
    hi = jnp.where(lane >= DA, scale, 0.0).astype(BF16)
    return lo, hi, lane < DA


def _softmax_pv(s_list, v_list):
    m = None
    for s in s_list:
        mx = jnp.max(s, axis=-1, keepdims=True)
        m = mx if m is None else jnp.maximum(m, mx)
    l = None
    o = None
    for s, v in zip(s_list, v_list):
        p = jnp.exp(s - m)
        ps = jnp.sum(p, axis=-1, keepdims=True)
        po = _dot(p.astype(BF16), v)
        l = ps if l is None else l + ps
        o = po if o is None else o + po
    return o / l


def _band_prompt_kernel(q_ref, k0_ref, k1_ref, k2_ref, v0_ref, v1_ref, v2_ref, bias_ref, o_ref):
    i = pl.program_id(1)
    lo, hi, is_lo = _pair_masks(DA ** -0.5)
    col_blk = lax.broadcasted_iota(jnp.int32, (1, 3 * Q_TILE), 1) // Q_TILE
    pad = jnp.where(col_blk + i < 2, NEG, 0.0).astype(F32)
    for pair in range(HA // 2):
        ls = slice(pair * LANES, (pair + 1) * LANES)
        kcat = jnp.concatenate([k0_ref[0, :, ls], k1_ref[0, :, ls], k2_ref[0, :, ls]], axis=0)
        vcat = jnp.concatenate([v0_ref[0, :, ls], v1_ref[0, :, ls], v2_ref[0, :, ls]], axis=0)
        for half in range(Q_TILE // A_HALF):
            q = q_ref[0, half * A_HALF:(half + 1) * A_HALF, ls]
            win = slice(half * A_HALF, half * A_HALF + A_WIN)
            k = kcat[win]
            v = vcat[win]
            outs = []
            for sub, msk in enumerate((lo, hi)):
                s = _dot_nt(q * msk, k) + bias_ref[2 * pair + sub] + pad[:, win]
                outs.append(_softmax_pv([s], [v]))
            o_ref[0, half * A_HALF:(half + 1) * A_HALF, ls] = (
                jnp.where(is_lo, outs[0], outs[1]).astype(o_ref.dtype))


def _band_prompt(q, k, v, bias):
    b, s, w = q.shape
    assert s % Q_TILE == 0
    qspec = pl.BlockSpec((1, Q_TILE, w), lambda bi, i: (bi, i, 0))

    def kspec(back):
        return pl.BlockSpec((1, Q_TILE, w), lambda bi, i: (bi, jnp.maximum(i - back, 0), 0))

    return pl.pallas_call(
        _band_prompt_kernel,
        grid=(b, s // Q_TILE),
        in_specs=[qspec, kspec(2), kspec(1), kspec(0), kspec(2), kspec(1), kspec(0),
                  _resident(bias.shape)],
        out_specs=qspec,
        out_shape=jax.ShapeDtypeStruct((b, s, w), BF16),
        compiler_params=_cparams("parallel", "parallel"),
        name="band_prompt",
    )(q, k, k, k, v, v, v, bias)


def _band_sample_kernel(q_ref, kc_ref, vc_ref, kn_ref, vn_ref, bias_c_ref, bias_n_ref, o_ref):
    lo, hi, is_lo = _pair_masks(DA ** -0.5)
    for bi in range(q_ref.shape[0]):
        for pair in range(HA // 2):
            ls = slice(pair * LANES, (pair + 1) * LANES)
            q = q_ref[bi, :, ls]
            kc = kc_ref[bi, :, ls].astype(BF16)
            vc = vc_ref[bi, :, ls].astype(BF16)
            kn = kn_ref[bi, :, ls]
            vn = vn_ref[bi, :, ls]
            outs = []
            for sub, msk in enumerate((lo, hi)):
                qm = q * msk
                sc = _dot_nt(qm, kc) + bias_c_ref[2 * pair + sub]
                sn = _dot_nt(qm, kn) + bias_n_ref[2 * pair + sub]
                outs.append(_softmax_pv([sc, sn], [vc, vn]))
            o_ref[bi, :, ls] = jnp.where(is_lo, outs[0], outs[1]).astype(o_ref.dtype)


def _band_sample(q, kn, vn, cache_k, cache_v, layer, bias_c, bias_n, group):
    b, t, w = q.shape
    a_len = cache_k.shape[2]
    assert b % group == 0
    new = pl.BlockSpec((group, t, w), lambda i: (i, 0, 0))
    old = pl.BlockSpec((None, group, a_len, w), lambda i: (layer, i, 0, 0))
    return pl.pallas_call(
        _band_sample_kernel,
        grid=(b // group,),
        in_specs=[new, old, old, new, new, _resident(bias_c.shape), _resident(bias_n.shape)],
        out_specs=new,
        out_shape=jax.ShapeDtypeStruct((b, t, w), BF16),
        compiler_params=_cparams("parallel"),
        name="band_sample",
    )(q, cache_k, cache_v, kn, vn, bias_c, bias_n)


def _lam(lq1_ref, lk1_ref, lq2_ref, lk2_ref, lam_init):
    s1 = jnp.sum(lq1_ref[...] * lk1_ref[...], axis=-1, keepdims=True)
    s2 = jnp.sum(lq2_ref[...] * lk2_ref[...], axis=-1, keepdims=True)
    return jnp.exp(s1) - jnp.exp(s2) + lam_init


def _diff_post(o, gsub, lam_init):
    return _rms(o, gsub) * (1.0 - lam_init)


def _diff_prompt_kernel(q_ref, k_ref, v_ref, lq1_ref, lk1_ref, lq2_ref, lk2_ref, gsub_ref,
                        o_ref, *, lam_init):
    i = pl.program_id(1)
    lam = _lam(lq1_ref, lk1_ref, lq2_ref, lk2_ref, lam_init)
    lo, hi, _ = _pair_masks(DB ** -0.5)
    qc = lax.broadcasted_iota(jnp.int32, (Q_TILE, Q_TILE), 0) // CHUNK
    kc = lax.broadcasted_iota(jnp.int32, (Q_TILE, Q_TILE), 1) // CHUNK
    diag_bias = jnp.where(kc <= qc, 0.0, NEG).astype(F32)

    for hd in range(HB):
        ls = slice(hd * EB, (hd + 1) * EB)
        q = q_ref[0, :, ls]
        q1 = q * lo
        q2 = q * hi

        def block(j, carry, bias):
            m1, l1, a1, m2, l2, a2 = carry
            rows = pl.ds(pl.multiple_of(j * Q_TILE, Q_TILE), Q_TILE)
            k = k_ref[0, rows, ls]
            v = v_ref[0, rows, ls]
            s1 = _dot_nt(q1, k)
            s2 = _dot_nt(q2, k)
            if bias is not None:
                s1 = s1 + bias
                s2 = s2 + bias
            n1 = jnp.maximum(m1, jnp.max(s1, axis=-1, keepdims=True))
            n2 = jnp.maximum(m2, jnp.max(s2, axis=-1, keepdims=True))
            p1 = jnp.exp(s1 - n1)
            p2 = jnp.exp(s2 - n2)
            c1 = jnp.exp(m1 - n1)
            c2 = jnp.exp(m2 - n2)
            pv = _dot(jnp.concatenate([p1, p2], axis=0).astype(BF16), v)
            return (n1, c1 * l1 + jnp.sum(p1, axis=-1, keepdims=True), c1 * a1 + pv[:Q_TILE],
                    n2, c2 * l2 + jnp.sum(p2, axis=-1, keepdims=True), c2 * a2 + pv[Q_TILE:])

        col = jnp.full((Q_TILE, 1), NEG, F32)
        zc = jnp.zeros((Q_TILE, 1), F32)
        za = jnp.zeros((Q_TILE, EB), F32)
        carry = lax.fori_loop(0, i, lambda j, c: block(j, c, None), (col, zc, za, col, zc, za))
        _, l1, a1, _, l2, a2 = block(i, carry, diag_bias)
        o = a1 / l1 - lam * (a2 / l2)
        o_ref[0, :, ls] = _diff_post(o, gsub_ref[...], lam_init).astype(o_ref.dtype)


def _diff_prompt(q, k, v, lams, gsub, lam_init):
    b, s, w = q.shape
    qspec = pl.BlockSpec((1, Q_TILE, w), lambda bi, i: (bi, i, 0))
    full = pl.BlockSpec((1, s, w), lambda bi, i: (bi, 0, 0))
    vec = _resident((1, DB))
    return pl.pallas_call(
        functools.partial(_diff_prompt_kernel, lam_init=lam_init),
        grid=(b, s // Q_TILE),
        in_specs=[qspec, full, full, vec, vec, vec, vec, _resident((1, EB))],
        out_specs=qspec,
        out_shape=jax.ShapeDtypeStruct((b, s, w), BF16),
        compiler_params=_cparams("parallel", "arbitrary"),
        name="diff_prompt",
    )(q, k, v, *lams, gsub)


def _diff_sample_kernel(q_ref, kc_ref, vc_ref, kn_ref, vn_ref, lq1_ref, lk1_ref, lq2_ref,
                        lk2_ref, gsub_ref, o_ref, *, lam_init):
    lam = _lam(lq1_ref, lk1_ref, lq2_ref, lk2_ref, lam_init)
    lo, hi, _ = _pair_masks(DB ** -0.5)
    for hd in range(HB):
        ls = slice(hd * EB, (hd + 1) * EB)
        q = q_ref[0, :, ls]
        kc = kc_ref[0, :, ls].astype(BF16)
        vc = vc_ref[0, :, ls].astype(BF16)
        kn = kn_ref[0, :, ls]
        vn = vn_ref[0, :, ls]
        outs = []
        for msk in (lo, hi):
            qm = q * msk
            outs.append(_softmax_pv([_dot_nt(qm, kc), _dot_nt(qm, kn)], [vc, vn]))
        o = outs[0] - lam * outs[1]
        o_ref[0, :, ls] = _diff_post(o, gsub_ref[...], lam_init).astype(o_ref.dtype)


def _diff_sample(q, kn, vn, cache_k, cache_v, layer, lams, gsub, lam_init):
    b, t, w = q.shape
    past = cache_k.shape[2]
    new = pl.BlockSpec((1, t, w), lambda i: (i, 0, 0))
    old = pl.BlockSpec((None, 1, past, w), lambda i: (layer, i, 0, 0))
    vec = _resident((1, DB))
    return pl.pallas_call(
        functools.partial(_diff_sample_kernel, lam_init=lam_init),
        grid=(b,),
        in_specs=[new, old, old, new, new, vec, vec, vec, vec, _resident((1, EB))],
        out_specs=new,
        out_shape=jax.ShapeDtypeStruct((b, t, w), BF16),
        compiler_params=_cparams("parallel"),
        name="diff_sample",
    )(q, cache_k, cache_v, kn, vn, *lams, gsub)


def _rope_tables(pos):
    half = DB // 2
    inv = ROPE_THETA ** (-jnp.arange(half, dtype=F32) * 2.0 / DB)
    ang = pos.astype(F32)[:, None] * inv[None, :]
    cos, sin = jnp.cos(ang), jnp.sin(ang)
    cos2 = jnp.concatenate([cos, cos], axis=-1)
    sin2 = jnp.concatenate([-sin, sin], axis=-1)
    return jnp.tile(cos2, (1, LANES // DB)), jnp.tile(sin2, (1, LANES // DB))


def _band_bias(table):
    q = jnp.arange(A_HALF)
    k = jnp.arange(A_WIN)
    idx = jnp.clip(BAND_PREV + q[:, None] - k[None, :], -MAX_REL, MAX_REL) + MAX_REL
    dc = k[None, :] // CHUNK - q[:, None] // CHUNK
    visible = (dc >= 0) & (dc <= PREV_CHUNKS)
    return jnp.where(visible[None], table.astype(F32)[:, idx], NEG)


def kernel(x_prompt, x_sample, cache_a_k, cache_a_v, cache_b_k, cache_b_v,
           g_ffn1, w1_gate, w1_up, w1_down, g_mix, w_in, g_qa, g_ka, g_qb, g_kb,
           rel_bias, lam_q1, lam_k1, lam_q2, lam_k2, g_sub, w_out,
           g_ffn2, w2_gate, w2_up, w2_down):
    bp, s, d = x_prompt.shape
    bd, t, _ = x_sample.shape
    depth = w_in.shape[0]
    past = cache_b_k.shape[2]
    a_len = cache_a_k.shape[2]
    assert a_len == BAND_PREV and t == CHUNK and s % Q_TILE == 0
    keep_p = min(BAND_PREV, s)

    cos_p, sin_p = _rope_tables(jnp.arange(s))
    reps = TOKEN_TILE // t
    cos_s, sin_s = (jnp.tile(a, (reps, 1)) for a in _rope_tables(past + jnp.arange(t)))

    cak = cache_a_k.reshape(depth, bd, a_len, GROUP_W)
    cav = cache_a_v.reshape(depth, bd, a_len, GROUP_W)
    cbk = cache_b_k.reshape(depth, bd, past, GROUP_W)
    cbv = cache_b_v.reshape(depth, bd, past, GROUP_W)

    yp = x_prompt.reshape(bp * s, d)
    ys = x_sample.reshape(bd * t, d)
    outs = [[] for _ in range(8)]
    for l in range(depth):
        lam_init = 0.8 - 0.6 * math.exp(-0.3 * l)
        row = lambda a: a[l].astype(F32).reshape(1, -1)
        tiled = lambda a: jnp.tile(row(a), (1, GROUP_W // a.shape[1]))
        w1 = (w1_gate[l].astype(BF16), w1_up[l].astype(BF16), w1_down[l].astype(BF16))
        w2 = (w2_gate[l].astype(BF16), w2_up[l].astype(BF16), w2_down[l].astype(BF16))
        wi = w_in[l].astype(BF16)
        woa = w_out[l, :GROUP_W].astype(BF16)
        wob = w_out[l, GROUP_W:].astype(BF16)
        gains = (tiled(g_qa), tiled(g_ka), tiled(g_qb), tiled(g_kb))
        lams = (row(lam_q1), row(lam_k1), row(lam_q2), row(lam_k2))
        gsub = row(g_sub)
        bias = _band_bias(rel_bias[l])
        bias_c = bias[:, :t, :a_len]
        bias_n = bias[:, :t, a_len:a_len + t]

        yp = _ffn(yp, row(g_ffn1), *w1)
        qa, ka, va, qb, kb, vb, ka32, va32, kb32, vb32 = _proj(
            yp, row(g_mix), wi, *gains, cos_p, sin_p, s)
        sh = lambda a: a.reshape(bp, s, GROUP_W)
        oa = _band_prompt(sh(qa), sh(ka), sh(va), bias)
        ob = _diff_prompt(sh(qb), sh(kb), sh(vb), lams, gsub, lam_init)
        yp = _out_ffn(yp, oa.reshape(bp * s, GROUP_W), ob.reshape(bp * s, GROUP_W),
                      woa, wob, row(g_ffn2), *w2)
        outs[0].append(ka32.reshape(bp, keep_p, HA, DA))
        outs[1].append(va32.reshape(bp, keep_p, HA, DA))
        outs[2].append(kb32.reshape(bp, s, HB, EB))
        outs[3].append(vb32.reshape(bp, s, HB, EB))

        ys = _ffn(ys, row(g_ffn1), *w1)
        qa, ka, va, qb, kb, vb, ka32, va32, kb32, vb32 = _proj(
            ys, row(g_mix), wi, *gains, cos_s, sin_s, t)
        sh = lambda a: a.reshape(bd, t, GROUP_W)
        oa = _band_sample(sh(qa), sh(ka), sh(va), cak, cav, l, bias_c, bias_n, 4)
        ob = _diff_sample(sh(qb), sh(kb), sh(vb), cbk, cbv, l, lams, gsub, lam_init)
        ys = _out_ffn(ys, oa.reshape(bd * t, GROUP_W), ob.reshape(bd * t, GROUP_W),
                      woa, wob, row(g_ffn2), *w2)
        outs[4].append(ka32.reshape(bd, t, HA, DA))
        outs[5].append(va32.reshape(bd, t, HA, DA))
        outs[6].append(kb32.reshape(bd, t, HB, EB))
        outs[7].append(vb32.reshape(bd, t, HB, EB))

    return (yp.reshape(bp, s, d), ys.reshape(bd, t, d)) + tuple(jnp.stack(o) for o in outs)
```

```python
import functools
import math

import jax
import jax.numpy as jnp
from jax import lax
from jax.experimental import pallas as pl
from jax.experimental.pallas import tpu as pltpu

CHUNK = 64
PREV_CHUNKS = 8
BAND_PREV = PREV_CHUNKS * CHUNK
MAX_REL = 128
HA = 8
DA = 64
HB = 4
DB = 64
EB = 2 * DB
GROUP_W = HA * DA
ROPE_THETA = 10000.0
EPS = 1e-6
NEG = -1e30

LANES = 128
MXU_DIM = 256
TOKEN_TILE = 512
Q_TILE = 256
A_WIN = BAND_PREV + LANES
A_GROUPS = A_WIN // LANES
REL_SPAN = A_WIN + LANES
VMEM_LIMIT = 56 * 1024 * 1024

F32 = jnp.float32
BF16 = jnp.bfloat16


def _cparams(*sem):
    return pltpu.CompilerParams(dimension_semantics=sem, vmem_limit_bytes=VMEM_LIMIT)


def _resident(shape):
    nd = len(shape)
    return pl.BlockSpec(shape, lambda *_: (0,) * nd, pipeline_mode=pl.Buffered(1))


def _dot(a, b):
    return jnp.dot(a, b, preferred_element_type=F32)


def _dot_nt(a, b):
    return lax.dot_general(a, b, (((1,), (1,)), ((), ())), preferred_element_type=F32)


def _rms(x, g):
    return x * lax.rsqrt(jnp.mean(x * x, axis=-1, keepdims=True) + EPS) * g


def _swiglu_half(x, g, wg_ref, wu_ref, wd_ref, f_chunk):
    h = _rms(x, g).astype(BF16)
    d_ff = wg_ref.shape[1]
    y = None
    for c in range(0, d_ff, f_chunk):
        gate = _dot(h, wg_ref[:, c:c + f_chunk])
        up = _dot(h, wu_ref[:, c:c + f_chunk])
        a = (gate / (1.0 + jnp.exp(-gate)) * up).astype(BF16)
        part = _dot(a, wd_ref[c:c + f_chunk, :])
        y = part if y is None else y + part
    return x + 0.5 * y


def _ffn_kernel(x_ref, g_ref, wg_ref, wu_ref, wd_ref, o_ref, *, f_chunk):
    o_ref[...] = _swiglu_half(x_ref[...], g_ref[...], wg_ref, wu_ref, wd_ref, f_chunk)


def _out_ffn_kernel(x_ref, oa_ref, ob_ref, woa_ref, wob_ref, g_ref, wg_ref, wu_ref, wd_ref,
                    o_ref, *, f_chunk):
    x = x_ref[...] + _dot(oa_ref[...], woa_ref[...]) + _dot(ob_ref[...], wob_ref[...])
    o_ref[...] = _swiglu_half(x, g_ref[...], wg_ref, wu_ref, wd_ref, f_chunk)


def _f_chunk(d_ff):
    return d_ff // 2 if d_ff % (2 * LANES) == 0 else d_ff


def _ffn(x, g, wg, wu, wd):
    n, d = x.shape
    d_ff = wg.shape[1]
    assert n % TOKEN_TILE == 0
    row = pl.BlockSpec((TOKEN_TILE, d), lambda i: (i, 0))
    return pl.pallas_call(
        functools.partial(_ffn_kernel, f_chunk=_f_chunk(d_ff)),
        grid=(n // TOKEN_TILE,),
        in_specs=[row, _resident((1, d)), _resident((d, d_ff)), _resident((d, d_ff)),
                  _resident((d_ff, d))],
        out_specs=row,
        out_shape=jax.ShapeDtypeStruct((n, d), F32),
        compiler_params=_cparams("parallel"),
        name="ffn",
    )(x, g, wg, wu, wd)


def _out_ffn(x, oa, ob, woa, wob, g, wg, wu, wd):
    n, d = x.shape
    d_ff = wg.shape[1]
    row = pl.BlockSpec((TOKEN_TILE, d), lambda i: (i, 0))
    grp = pl.BlockSpec((TOKEN_TILE, GROUP_W), lambda i: (i, 0))
    return pl.pallas_call(
        functools.partial(_out_ffn_kernel, f_chunk=_f_chunk(d_ff)),
        grid=(n // TOKEN_TILE,),
        in_specs=[row, grp, grp, _resident((GROUP_W, d)), _resident((GROUP_W, d)),
                  _resident((1, d)), _resident((d, d_ff)), _resident((d, d_ff)),
                  _resident((d_ff, d))],
        out_specs=row,
        out_shape=jax.ShapeDtypeStruct((n, d), F32),
        compiler_params=_cparams("parallel"),
        name="out_ffn",
    )(x, oa, ob, woa, wob, g, wg, wu, wd)


def _head_norm(z, gain, ones_bd):
    outs = []
    for c in range(0, z.shape[1], MXU_DIM):
        zc = z[:, c:c + MXU_DIM]
        sq = zc * zc
        hi = sq.astype(BF16)
        lo = (sq - hi.astype(F32)).astype(BF16)
        ss = _dot(hi, ones_bd) + _dot(lo, ones_bd)
        outs.append(zc * lax.rsqrt(ss * (1.0 / DA) + EPS) * gain[:, c:c + MXU_DIM])
    return jnp.concatenate(outs, axis=1)


def _rope(y, cos2, sin2):
    first_half = (lax.broadcasted_iota(jnp.int32, (1, LANES), 1) % DB) < (DB // 2)
    outs = []
    for c in range(0, y.shape[1], LANES):
        yc = y[:, c:c + LANES]
        swapped = jnp.where(first_half, pltpu.roll(yc, LANES - DB // 2, 1),
                            pltpu.roll(yc, DB // 2, 1))
        outs.append(yc * cos2 + swapped * sin2)
    return jnp.concatenate(outs, axis=1)


def _store_heads_interleaved(o_ref, val):
    m = val.shape[0]
    for hd in range(HB):
        o_ref[pl.ds(hd, m, stride=HB), :] = val[:, hd * EB:(hd + 1) * EB]


def _proj_kernel(*refs, tiles_per_seq, transposed, aliased):
    (x_ref, g_ref, w_ref, gqa_ref, gka_ref, gqb_ref, gkb_ref, cos_ref, sin_ref) = refs[:9]
    outs = refs[9 + (2 if aliased else 0):]
    qa_o, ka_o, va_o, qb_o, kb_o, vb_o, ka32_o, va32_o, kb32_o, vb32_o = outs
    h = _rms(x_ref[...], g_ref[...]).astype(BF16)
    r = lax.broadcasted_iota(jnp.int32, (MXU_DIM, MXU_DIM), 0) // DA
    c = lax.broadcasted_iota(jnp.int32, (MXU_DIM, MXU_DIM), 1) // DA
    ones_bd = (r == c).astype(BF16)
    cos2 = cos_ref[...]
    sin2 = sin_ref[...]

    def group(j):
        return _dot(h, w_ref[:, j * GROUP_W:(j + 1) * GROUP_W])

    def maybe_t(a):
        return a.T if transposed else a

    qa_o[...] = maybe_t(_head_norm(group(0), gqa_ref[...], ones_bd)).astype(BF16)
    ka = _head_norm(group(1), gka_ref[...], ones_bd)
    va = group(2)
    va_t = maybe_t(va)
    ka_o[...] = ka.astype(BF16)
    va_o[...] = va_t.astype(BF16)
    qb = _rope(_head_norm(group(3), gqb_ref[...], ones_bd), cos2, sin2)
    qb_o[...] = maybe_t(qb).astype(BF16)
    kb = _rope(_head_norm(group(4), gkb_ref[...], ones_bd), cos2, sin2)
    vb = group(5)
    kb_o[...] = kb.astype(BF16)
    vb_o[...] = maybe_t(vb).astype(BF16)
    _store_heads_interleaved(kb32_o, kb)
    _store_heads_interleaved(vb32_o, vb)

    @pl.when(pl.program_id(0) % tiles_per_seq == tiles_per_seq - 1)
    def _():
        ka32_o[...] = maybe_t(ka)
        va32_o[...] = va_t


def _proj(x, g, w_in, gqa, gka, gqb, gkb, cos2, sin2, seq, layer, depth, transposed, prev_b):
    n, d = x.shape
    keep = min(BAND_PREV, seq)
    if seq >= TOKEN_TILE:
        assert seq % TOKEN_TILE == 0 and keep == TOKEN_TILE
        tiles_per_seq = seq // TOKEN_TILE
    else:
        assert TOKEN_TILE % seq == 0 and keep == seq and not transposed
        tiles_per_seq = 1
    n_seq = n // seq
    pos_tiles = cos2.shape[0] // TOKEN_TILE
    row = pl.BlockSpec((TOKEN_TILE, d), lambda i: (i, 0))
    grp = pl.BlockSpec((TOKEN_TILE, GROUP_W), lambda i: (i, 0))
    grp_t = pl.BlockSpec((GROUP_W, TOKEN_TILE), lambda i: (0, i))
    tab = pl.BlockSpec((TOKEN_TILE, LANES), lambda i: (i % pos_tiles, 0))
    gain = _resident((1, GROUP_W))
    nat = jax.ShapeDtypeStruct((n, GROUP_W), BF16)
    tr = jax.ShapeDtypeStruct((GROUP_W, n), BF16)
    if transposed:
        qv_spec, qv_shape = grp_t, tr
        kept = pl.BlockSpec((None, GROUP_W, keep), lambda i: (i // tiles_per_seq, 0, 0))
        kept_shape = jax.ShapeDtypeStruct((n_seq, GROUP_W, keep), F32)
    else:
        qv_spec, qv_shape = grp, nat
        kept = pl.BlockSpec((TOKEN_TILE, GROUP_W), lambda i: (i // tiles_per_seq, 0))
        kept_shape = jax.ShapeDtypeStruct((n_seq * keep, GROUP_W), F32)
    cache_b = pl.BlockSpec((None, TOKEN_TILE * HB, EB), lambda i: (layer, i, 0))
    cache_b_shape = jax.ShapeDtypeStruct((depth, n * HB, EB), F32)
    aliased = prev_b is not None
    any_spec = pl.BlockSpec(memory_space=pl.ANY)
    return pl.pallas_call(
        functools.partial(_proj_kernel, tiles_per_seq=tiles_per_seq, transposed=transposed,
                          aliased=aliased),
        grid=(n // TOKEN_TILE,),
        in_specs=[row, _resident((1, d)), _resident(w_in.shape), gain, gain, gain, gain, tab, tab]
                 + ([any_spec, any_spec] if aliased else []),
        out_specs=[qv_spec, grp, qv_spec, qv_spec, grp, qv_spec, kept, kept, cache_b, cache_b],
        out_shape=[qv_shape, nat, qv_shape, qv_shape, nat, qv_shape, kept_shape, kept_shape,
                   cache_b_shape, cache_b_shape],
        input_output_aliases={9: 8, 10: 9} if aliased else {},
        compiler_params=_cparams("arbitrary"),
        name="proj",
    )(x, g, w_in, gqa, gka, gqb, gkb, cos2, sin2, *(prev_b if aliased else ()))


def _row_half_masks(scale, cols=LANES):
    row = lax.broadcasted_iota(jnp.int32, (LANES, cols), 0)
    top = jnp.where(row < LANES // 2, scale, 0.0).astype(BF16)
    bottom = jnp.where(row >= LANES // 2, scale, 0.0).astype(BF16)
    return top, bottom


def _toeplitz(tab_c, tab_prev, rows, base_shift):
    lane = lax.broadcasted_iota(jnp.int32, (rows, LANES), 1)
    shift = lax.broadcasted_iota(jnp.int32, (rows, LANES), 0) + base_shift
    cur = pltpu.roll(jnp.broadcast_to(tab_c, (rows, LANES)), base_shift, 1, stride=1, stride_axis=0)
    prev = pltpu.roll(jnp.broadcast_to(tab_prev, (rows, LANES)), base_shift, 1, stride=1,
                      stride_axis=0)
    return jnp.where(lane >= shift, cur, prev)


def _band_prompt_kernel(qt_ref, k0_ref, k1_ref, k2_ref, v0_ref, v1_ref, v2_ref, rel_ref, o_ref,
                        bias_ref):
    i = pl.program_id(1)

    @pl.when((pl.program_id(0) == 0) & (i == 0))
    def _():
        kc = lax.broadcasted_iota(jnp.int32, (LANES, LANES), 0) // CHUNK
        qc = lax.broadcasted_iota(jnp.int32, (LANES, LANES), 1) // CHUNK
        for g in range(A_GROUPS):
            dc = kc + (LANES // CHUNK) * g - qc
            visible = (dc >= 0) & (dc <= PREV_CHUNKS)
            hi = (A_GROUPS - 1 - g) * LANES
            for hd in range(HA):
                t = _toeplitz(rel_ref[hd:hd + 1, hi + LANES:hi + 2 * LANES],
                              rel_ref[hd:hd + 1, hi:hi + LANES], LANES, 1)
                bias_ref[hd, g * LANES:(g + 1) * LANES, :] = jnp.where(visible, t, NEG)
        bias_ref[HA] = jnp.full((A_WIN, LANES), NEG, F32)

    top, bottom = _row_half_masks(DA ** -0.5)
    k_refs = (k0_ref, k1_ref, k2_ref)
    v_refs = (v0_ref, v1_ref, v2_ref)
    halves = Q_TILE // LANES
    items = [(pair, half) for pair in range(HA // 2) for half in range(halves)]

    def scores(pair, half):
        ls = slice(pair * LANES, (pair + 1) * LANES)
        kwin = jnp.concatenate([r[:, ls] for r in k_refs], axis=0)[half * LANES:half * LANES + A_WIN]
        qt = qt_ref[ls, half * LANES:(half + 1) * LANES]
        return _dot(kwin, jnp.concatenate([qt * top, qt * bottom], axis=1))

    def finish(pair, half, st2):
        ls = slice(pair * LANES, (pair + 1) * LANES)
        ps, sums = [], []
        for sub in range(2):
            hd = 2 * pair + sub
            parts = []
            for g in range(A_GROUPS):
                idx = jnp.where(2 * i + half + g >= A_GROUPS - 1, hd, HA)
                parts.append(st2[g * LANES:(g + 1) * LANES, sub * LANES:(sub + 1) * LANES]
                             + bias_ref[idx, g * LANES:(g + 1) * LANES, :])
            s = jnp.concatenate(parts, axis=0)
            p = jnp.exp(s - jnp.max(s, axis=0, keepdims=True))
            sums.append(jnp.sum(p, axis=0, keepdims=True))
            ps.append(p.astype(BF16))
        vwin = jnp.concatenate([r[ls, :] for r in v_refs], axis=1)[:, half * LANES:half * LANES + A_WIN]
        ot2 = _dot(vwin, jnp.concatenate(ps, axis=1))
        ot = jnp.concatenate([ot2[:DA, :LANES] / sums[0], ot2[DA:, LANES:] / sums[1]], axis=0)
        o_ref[half * LANES:(half + 1) * LANES, ls] = ot.T.astype(o_ref.dtype)

    st2 = scores(*items[0])
    for n, item in enumerate(items):
        nxt = scores(*items[n + 1]) if n + 1 < len(items) else None
        finish(*item, st2)
        st2 = nxt


def _band_prompt(qt, k, vt, rel, bsz):
    w, n = qt.shape
    tiles = n // bsz // Q_TILE
    qspec = pl.BlockSpec((w, Q_TILE), lambda b, i: (0, b * tiles + i))
    ospec = pl.BlockSpec((Q_TILE, w), lambda b, i: (b * tiles + i, 0))

    def kspec(back):
        return pl.BlockSpec((Q_TILE, w), lambda b, i: (b * tiles + jnp.maximum(i - back, 0), 0))

    def vspec(back):
        return pl.BlockSpec((w, Q_TILE), lambda b, i: (0, b * tiles + jnp.maximum(i - back, 0)))

    return pl.pallas_call(
        _band_prompt_kernel,
        grid=(bsz, tiles),
        in_specs=[qspec, kspec(2), kspec(1), kspec(0), vspec(2), vspec(1), vspec(0),
                  _resident(rel.shape)],
        out_specs=ospec,
        out_shape=jax.ShapeDtypeStruct((n, w), BF16),
        scratch_shapes=[pltpu.VMEM((HA + 1, A_WIN, LANES), F32)],
        compiler_params=_cparams("arbitrary", "arbitrary"),
        name="band_prompt",
    )(qt, k, k, k, vt, vt, vt, rel)


def _pair_masks(scale):
    lane = lax.broadcasted_iota(jnp.int32, (1, LANES), 1)
    lo = jnp.where(lane < DA, scale, 0.0).astype(BF16)
    hi = jnp.where(lane >= DA, scale, 0.0).astype(BF16)
    return lo, hi, lane < DA


def _softmax_pv(s_list, pv_list):
    m = None
    for s in s_list:
        mx = jnp.max(s, axis=-1, keepdims=True)
        m = mx if m is None else jnp.maximum(m, mx)
    l = None
    o = None
    for s, pv in zip(s_list, pv_list):
        p = jnp.exp(s - m)
        ps = jnp.sum(p, axis=-1, keepdims=True)
        po = pv(p.astype(BF16))
        l = ps if l is None else l + ps
        o = po if o is None else o + po
    return o / l


def _band_sample_kernel(q_ref, kct_ref, vct_ref, kn_ref, vn_ref, rel_ref, o_ref, bias_ref):
    t = q_ref.shape[1]

    @pl.when(pl.program_id(0) == 0)
    def _():
        for g in range(A_GROUPS):
            lo = (g + 1) * LANES
            for hd in range(HA):
                bias_ref[hd, :, g * LANES:(g + 1) * LANES] = _toeplitz(
                    rel_ref[hd:hd + 1, lo:lo + LANES], rel_ref[hd:hd + 1, lo - LANES:lo], t, 0)

    lo, hi, is_lo = _pair_masks(DA ** -0.5)
    for bi in range(q_ref.shape[0]):
        for pair in range(HA // 2):
            ls = slice(pair * LANES, (pair + 1) * LANES)
            q = q_ref[bi, :, ls]
            kct = kct_ref[bi, 2 * pair:2 * pair + 2].reshape(LANES, BAND_PREV).astype(BF16)
            vct = vct_ref[bi, 2 * pair:2 * pair + 2].reshape(LANES, BAND_PREV).astype(BF16)
            kn = kn_ref[bi, :, ls]
            vn = vn_ref[bi, :, ls]
            outs = []
            for sub, msk in enumerate((lo, hi)):
                qm = q * msk
                sc = _dot(qm, kct) + bias_ref[2 * pair + sub, :, :BAND_PREV]
                sn = _dot_nt(qm, kn) + bias_ref[2 * pair + sub, :, BAND_PREV:BAND_PREV + t]
                outs.append(_softmax_pv([sc, sn], [lambda p: _dot_nt(p, vct),
                                                   lambda p: _dot(p, vn)]))
            o_ref[bi, :, ls] = jnp.where(is_lo, outs[0], outs[1]).astype(o_ref.dtype)


def _band_sample(q, kn, vn, cache_kt, cache_vt, layer, rel, group):
    b, t, w = q.shape
    a_len = cache_kt.shape[4]
    assert b % group == 0 and a_len == BAND_PREV and t == CHUNK
    new = pl.BlockSpec((group, t, w), lambda i: (i, 0, 0))
    old = pl.BlockSpec((None, group, HA, DA, a_len), lambda i: (layer, i, 0, 0, 0))
    return pl.pallas_call(
        _band_sample_kernel,
        grid=(b // group,),
        in_specs=[new, old, old, new, new, _resident(rel.shape)],
        out_specs=new,
        out_shape=jax.ShapeDtypeStruct((b, t, w), BF16),
        scratch_shapes=[pltpu.VMEM((HA, t, A_WIN), F32)],
        compiler_params=_cparams("arbitrary"),
        name="band_sample",
    )(q, cache_kt, cache_vt, kn, vn, rel)


def _lam(lq1_ref, lk1_ref, lq2_ref, lk2_ref, lam_init):
    s1 = jnp.sum(lq1_ref[...] * lk1_ref[...], axis=-1, keepdims=True)
    s2 = jnp.sum(lq2_ref[...] * lk2_ref[...], axis=-1, keepdims=True)
    return jnp.exp(s1) - jnp.exp(s2) + lam_init


def _diff_prompt_kernel(qt_ref, k_ref, vt_ref, lq1_ref, lk1_ref, lq2_ref, lk2_ref, gsub_ref,
                        o_ref, m_ref, l_ref, acc_ref, q2_ref, st_ref, *, lam_init):
    i = pl.program_id(1)
    lam = _lam(lq1_ref, lk1_ref, lq2_ref, lk2_ref, lam_init)
    top, bottom = _row_half_masks(DB ** -0.5, Q_TILE)
    kc = lax.broadcasted_iota(jnp.int32, (Q_TILE, 2 * Q_TILE), 0) // CHUNK
    qc = (lax.broadcasted_iota(jnp.int32, (Q_TILE, 2 * Q_TILE), 1) % Q_TILE) // CHUNK
    diag_bias = jnp.where(kc <= qc, 0.0, NEG).astype(F32)

    for hd in range(HB):
        qt = qt_ref[hd * EB:(hd + 1) * EB, :]
        q2_ref[hd] = jnp.concatenate([qt * top, qt * bottom], axis=1)
    m_ref[...] = jnp.full(m_ref.shape, NEG, F32)
    l_ref[...] = jnp.zeros(l_ref.shape, F32)
    acc_ref[...] = jnp.zeros(acc_ref.shape, F32)

    def key_rows(j):
        return pl.ds(pl.multiple_of(j * Q_TILE, Q_TILE), Q_TILE)

    def scores(j, hd):
        return _dot(k_ref[key_rows(j), hd * EB:(hd + 1) * EB], q2_ref[hd])

    def update(j, hd, st, bias):
        if bias is not None:
            st = st + bias
        m_old = m_ref[hd]
        m_new = jnp.maximum(m_old, jnp.max(st, axis=0, keepdims=True))
        p = jnp.exp(st - m_new)
        alpha = jnp.exp(m_old - m_new)
        m_ref[hd] = m_new
        l_ref[hd] = alpha * l_ref[hd] + jnp.sum(p, axis=0, keepdims=True)
        pv = _dot(vt_ref[hd * EB:(hd + 1) * EB, key_rows(j)], p.astype(BF16))
        acc_ref[hd] = alpha * acc_ref[hd] + pv

    st_ref[...] = scores(0, 0)

    def body(j, carry):
        st = st_ref[...]
        for hd in range(HB):
            nxt = scores(j, hd + 1) if hd + 1 < HB else scores(j + 1, 0)
            update(j, hd, st, None)
            st = nxt
        st_ref[...] = st
        return carry

    lax.fori_loop(0, i, body, 0)
    st = st_ref[...]
    for hd in range(HB):
        nxt = scores(i, hd + 1) if hd + 1 < HB else None
        update(i, hd, st, diag_bias)
        st = nxt

    for hd in range(HB):
        o12 = acc_ref[hd] / l_ref[hd]
        o = o12[:, :Q_TILE] - lam * o12[:, Q_TILE:]
        o = o * lax.rsqrt(jnp.mean(o * o, axis=0, keepdims=True) + EPS) * gsub_ref[...]
        o_ref[:, hd * EB:(hd + 1) * EB] = (o * (1.0 - lam_init)).T.astype(o_ref.dtype)


def _diff_prompt(qt, k, vt, lams, gsub_col, lam_init, bsz):
    w, n = qt.shape
    s = n // bsz
    tiles = s // Q_TILE
    qspec = pl.BlockSpec((w, Q_TILE), lambda b, i: (0, b * tiles + i))
    ospec = pl.BlockSpec((Q_TILE, w), lambda b, i: (b * tiles + i, 0))
    kfull = pl.BlockSpec((s, w), lambda b, i: (b, 0))
    vfull = pl.BlockSpec((w, s), lambda b, i: (0, b))
    vec = _resident((1, DB))
    return pl.pallas_call(
        functools.partial(_diff_prompt_kernel, lam_init=lam_init),
        grid=(bsz, tiles),
        in_specs=[qspec, kfull, vfull, vec, vec, vec, vec, _resident((EB, 1))],
        out_specs=ospec,
        out_shape=jax.ShapeDtypeStruct((n, w), BF16),
        scratch_shapes=[pltpu.VMEM((HB, 1, 2 * Q_TILE), F32), pltpu.VMEM((HB, 1, 2 * Q_TILE), F32),
                        pltpu.VMEM((HB, EB, 2 * Q_TILE), F32),
                        pltpu.VMEM((HB, EB, 2 * Q_TILE), BF16),
                        pltpu.VMEM((Q_TILE, 2 * Q_TILE), F32)],
        compiler_params=_cparams("parallel", "arbitrary"),
        name="diff_prompt",
    )(qt, k, vt, *lams, gsub_col)


def _diff_sample_kernel(q_ref, kc_ref, vc_ref, kn_ref, vn_ref, lq1_ref, lk1_ref, lq2_ref,
                        lk2_ref, gsub_ref, o_ref, *, lam_init):
    lam = _lam(lq1_ref, lk1_ref, lq2_ref, lk2_ref, lam_init)
    lo, hi, _ = _pair_masks(DB ** -0.5)
    past = kc_ref.shape[0] // HB
    for hd in range(HB):
        ls = slice(hd * EB, (hd + 1) * EB)
        q = q_ref[0, :, ls]
        kc = kc_ref[pl.ds(hd, past, stride=HB), :].astype(BF16)
        vc = vc_ref[pl.ds(hd, past, stride=HB), :].astype(BF16)
        kn = kn_ref[0, :, ls]
        vn = vn_ref[0, :, ls]
        outs = []
        for msk in (lo, hi):
            qm = q * msk
            outs.append(_softmax_pv([_dot_nt(qm, kc), _dot_nt(qm, kn)],
                                    [lambda p: _dot(p, vc), lambda p: _dot(p, vn)]))
        o = outs[0] - lam * outs[1]
        o_ref[0, :, ls] = (_rms(o, gsub_ref[...]) * (1.0 - lam_init)).astype(o_ref.dtype)


def _diff_sample(q, kn, vn, cache_k, cache_v, layer, lams, gsub, lam_init):
    b, t, w = q.shape
    rows = cache_k.shape[2]
    new = pl.BlockSpec((1, t, w), lambda i: (i, 0, 0))
    old = pl.BlockSpec((None, None, rows, EB), lambda i: (layer, i, 0, 0))
    vec = _resident((1, DB))
    return pl.pallas_call(
        functools.partial(_diff_sample_kernel, lam_init=lam_init),
        grid=(b,),
        in_specs=[new, old, old, new, new, vec, vec, vec, vec, _resident((1, EB))],
        out_specs=new,
        out_shape=jax.ShapeDtypeStruct((b, t, w), BF16),
        compiler_params=_cparams("parallel"),
        name="diff_sample",
    )(q, cache_k, cache_v, kn, vn, *lams, gsub)


def _rope_tables(pos):
    half = DB // 2
    inv = ROPE_THETA ** (-jnp.arange(half, dtype=F32) * 2.0 / DB)
    ang = pos.astype(F32)[:, None] * inv[None, :]
    cos, sin = jnp.cos(ang), jnp.sin(ang)
    cos2 = jnp.concatenate([cos, cos], axis=-1)
    sin2 = jnp.concatenate([-sin, sin], axis=-1)
    return jnp.tile(cos2, (1, LANES // DB)), jnp.tile(sin2, (1, LANES // DB))


def _rel_rows(table):
    table = table.astype(F32)
    n_const = REL_SPAN - (2 * MAX_REL - 1)
    const = jnp.broadcast_to(table[:, 2 * MAX_REL:], (HA, n_const))
    desc = jnp.concatenate([table[:, 1:2 * MAX_REL], const], axis=1)
    asc = jnp.concatenate([const, table[:, 1:2 * MAX_REL][:, ::-1]], axis=1)
    return asc, desc


def kernel(x_prompt, x_sample, cache_a_k, cache_a_v, cache_b_k, cache_b_v,
           g_ffn1, w1_gate, w1_up, w1_down, g_mix, w_in, g_qa, g_ka, g_qb, g_kb,
           rel_bias, lam_q1, lam_k1, lam_q2, lam_k2, g_sub, w_out,
           g_ffn2, w2_gate, w2_up, w2_down):
    bp, s, d = x_prompt.shape
    bd, t, _ = x_sample.shape
    depth = w_in.shape[0]
    past = cache_b_k.shape[2]
    a_len = cache_a_k.shape[2]
    assert a_len == BAND_PREV and t == CHUNK and s % Q_TILE == 0 and MAX_REL == LANES
    keep_p = min(BAND_PREV, s)

    cos_p, sin_p = _rope_tables(jnp.arange(s))
    reps = TOKEN_TILE // t
    cos_s, sin_s = (jnp.tile(a, (reps, 1)) for a in _rope_tables(past + jnp.arange(t)))

    cakt = jnp.transpose(cache_a_k, (0, 1, 3, 4, 2))
    cavt = jnp.transpose(cache_a_v, (0, 1, 3, 4, 2))
    cbk = cache_b_k.reshape(depth, bd, past * HB, EB)
    cbv = cache_b_v.reshape(depth, bd, past * HB, EB)

    yp = x_prompt.reshape(bp * s, d)
    ys = x_sample.reshape(bd * t, d)
    pak, pav, sak, sav = [], [], [], []
    pb = sb = None
    for l in range(depth):
        lam_init = 0.8 - 0.6 * math.exp(-0.3 * l)
        row = lambda a: a[l].astype(F32).reshape(1, -1)
        tiled = lambda a: jnp.tile(row(a), (1, GROUP_W // a.shape[1]))
        w1 = (w1_gate[l].astype(BF16), w1_up[l].astype(BF16), w1_down[l].astype(BF16))
        w2 = (w2_gate[l].astype(BF16), w2_up[l].astype(BF16), w2_down[l].astype(BF16))
        wi = w_in[l].astype(BF16)
        woa = w_out[l, :GROUP_W].astype(BF16)
        wob = w_out[l, GROUP_W:].astype(BF16)
        gains = (tiled(g_qa), tiled(g_ka), tiled(g_qb), tiled(g_kb))
        lams = (row(lam_q1), row(lam_k1), row(lam_q2), row(lam_k2))
        gsub = row(g_sub)
        rel_asc, rel_desc = _rel_rows(rel_bias[l])

        yp = _ffn(yp, row(g_ffn1), *w1)
        qat, ka, vat, qbt, kb, vbt, kat32, vat32, *pb = _proj(
            yp, row(g_mix), wi, *gains, cos_p, sin_p, s, l, depth, True, pb)
        oa = _band_prompt(qat, ka, vat, rel_desc, bp)
        ob = _diff_prompt(qbt, kb, vbt, lams, gsub.reshape(EB, 1), lam_init, bp)
        yp = _out_ffn(yp, oa, ob, woa, wob, row(g_ffn2), *w2)
        pak.append(kat32)
        pav.append(vat32)

        ys = _ffn(ys, row(g_ffn1), *w1)
        qa, ka, va, qb, kb, vb, ka32, va32, *sb = _proj(
            ys, row(g_mix), wi, *gains, cos_s, sin_s, t, l, depth, False, sb)
        sh = lambda a: a.reshape(bd, t, GROUP_W)
        oa = _band_sample(sh(qa), sh(ka), sh(va), cakt, cavt, l, rel_asc, 4)
        ob = _diff_sample(sh(qb), sh(kb), sh(vb), cbk, cbv, l, lams, gsub, lam_init)
        ys = _out_ffn(ys, oa.reshape(bd * t, GROUP_W), ob.reshape(bd * t, GROUP_W),
                      woa, wob, row(g_ffn2), *w2)
        sak.append(ka32)
        sav.append(va32)

    def kept_prompt(parts):
        a = jnp.stack(parts).reshape(depth, bp, HA, DA, keep_p)
        return jnp.transpose(a, (0, 1, 4, 2, 3))

    return (yp.reshape(bp, s, d), ys.reshape(bd, t, d),
            kept_prompt(pak), kept_prompt(pav),
            pb[0].reshape(depth, bp, s, HB, EB), pb[1].reshape(depth, bp, s, HB, EB),
            jnp.stack(sak).reshape(depth, bd, t, HA, DA),
            jnp.stack(sav).reshape(depth, bd, t, HA, DA),
            sb[0].reshape(depth, bd, t, HB, EB), sb[1].reshape(depth, bd, t, HB, EB))
```

```python
import functools
import math

import jax
import jax.numpy as jnp
from jax import lax
from jax.experimental import pallas as pl
from jax.experimental.pallas import tpu as pltpu

CHUNK = 64
PREV_CHUNKS = 8
BAND_PREV = PREV_CHUNKS * CHUNK
MAX_REL = 128
HA = 8
DA = 64
HB = 4
DB = 64
EB = 2 * DB
GROUP_W = HA * DA
ROPE_THETA = 10000.0
EPS = 1e-6
NEG = -1e30
LOG2E = math.log2(math.e)

LANES = 128
MXU_DIM = 256
TOKEN_TILE = 512
Q_TILE = 256
A_WIN = BAND_PREV + LANES
A_GROUPS = A_WIN // LANES
REL_SPAN = A_WIN + LANES
A_LOOKAHEAD = 3
B_LOOKAHEAD = 2
VMEM_LIMIT = 56 * 1024 * 1024

F32 = jnp.float32
BF16 = jnp.bfloat16


def _cparams(*sem):
    return pltpu.CompilerParams(dimension_semantics=sem, vmem_limit_bytes=VMEM_LIMIT)


def _resident(shape):
    nd = len(shape)
    return pl.BlockSpec(shape, lambda *_: (0,) * nd, pipeline_mode=pl.Buffered(1))


def _dot(a, b):
    return jnp.dot(a, b, preferred_element_type=F32)


def _dot_nt(a, b):
    return lax.dot_general(a, b, (((1,), (1,)), ((), ())), preferred_element_type=F32)


def _rms(x, g):
    return x * lax.rsqrt(jnp.mean(x * x, axis=-1, keepdims=True) + EPS) * g


def _swiglu_half(x, g, wg_ref, wu_ref, wd_ref):
    h = _rms(x, g).astype(BF16)
    d_ff = wg_ref.shape[1]
    split = min(d_ff, pl.cdiv(d_ff, 2 * MXU_DIM) * MXU_DIM)
    y = None
    for lo, hi in ((0, split), (split, d_ff)):
        if lo == hi:
            continue
        gate = _dot(h, wg_ref[:, lo:hi])
        up = _dot(h, wu_ref[:, lo:hi])
        a = (gate / (1.0 + jnp.exp(-gate)) * up).astype(BF16)
        part = _dot(a, wd_ref[lo:hi, :])
        y = part if y is None else y + part
    return x + 0.5 * y


def _ffn_kernel(x_ref, g_ref, wg_ref, wu_ref, wd_ref, o_ref):
    o_ref[...] = _swiglu_half(x_ref[...], g_ref[...], wg_ref, wu_ref, wd_ref)


def _out_ffn_kernel(x_ref, oa_ref, ob_ref, woa_ref, wob_ref, g_ref, wg_ref, wu_ref, wd_ref,
                    o_ref):
    x = x_ref[...] + _dot(oa_ref[...], woa_ref[...]) + _dot(ob_ref[...], wob_ref[...])
    o_ref[...] = _swiglu_half(x, g_ref[...], wg_ref, wu_ref, wd_ref)


def _ffn(x, g, wg, wu, wd):
    n, d = x.shape
    d_ff = wg.shape[1]
    assert n % TOKEN_TILE == 0
    row = pl.BlockSpec((TOKEN_TILE, d), lambda i: (i, 0))
    return pl.pallas_call(
        _ffn_kernel,
        grid=(n // TOKEN_TILE,),
        in_specs=[row, _resident((1, d)), _resident((d, d_ff)), _resident((d, d_ff)),
                  _resident((d_ff, d))],
        out_specs=row,
        out_shape=jax.ShapeDtypeStruct((n, d), F32),
        compiler_params=_cparams("parallel"),
        name="ffn",
    )(x, g, wg, wu, wd)


def _out_ffn(x, oa, ob, woa, wob, g, wg, wu, wd):
    n, d = x.shape
    d_ff = wg.shape[1]
    row = pl.BlockSpec((TOKEN_TILE, d), lambda i: (i, 0))
    grp = pl.BlockSpec((TOKEN_TILE, GROUP_W), lambda i: (i, 0))
    return pl.pallas_call(
        _out_ffn_kernel,
        grid=(n // TOKEN_TILE,),
        in_specs=[row, grp, grp, _resident((GROUP_W, d)), _resident((GROUP_W, d)),
                  _resident((1, d)), _resident((d, d_ff)), _resident((d, d_ff)),
                  _resident((d_ff, d))],
        out_specs=row,
        out_shape=jax.ShapeDtypeStruct((n, d), F32),
        compiler_params=_cparams("parallel"),
        name="out_ffn",
    )(x, oa, ob, woa, wob, g, wg, wu, wd)


def _head_norm(z, gain, ones_bd):
    outs = []
    for c in range(0, z.shape[1], MXU_DIM):
        zc = z[:, c:c + MXU_DIM]
        sq = zc * zc
        hi = sq.astype(BF16)
        lo = (sq - hi.astype(F32)).astype(BF16)
        ss = _dot(hi, ones_bd) + _dot(lo, ones_bd)
        outs.append(zc * lax.rsqrt(ss * (1.0 / DA) + EPS) * gain[:, c:c + MXU_DIM])
    return jnp.concatenate(outs, axis=1)


def _rope(y, cos2, sin2):
    first_half = (lax.broadcasted_iota(jnp.int32, (1, LANES), 1) % DB) < (DB // 2)
    outs = []
    for c in range(0, y.shape[1], LANES):
        yc = y[:, c:c + LANES]
        swapped = jnp.where(first_half, pltpu.roll(yc, LANES - DB // 2, 1),
                            pltpu.roll(yc, DB // 2, 1))
        outs.append(yc * cos2 + swapped * sin2)
    return jnp.concatenate(outs, axis=1)


def _store_heads_interleaved(o_ref, val):
    m = val.shape[0]
    for hd in range(HB):
        o_ref[pl.ds(hd, m, stride=HB), :] = val[:, hd * EB:(hd + 1) * EB]


def _proj_kernel(*refs, tiles_per_seq, transposed, aliased):
    (x_ref, g_ref, w_ref, gqa_ref, gka_ref, gqb_ref, gkb_ref, cos_ref, sin_ref) = refs[:9]
    outs = refs[9 + (2 if aliased else 0):]
    qa_o, ka_o, va_o, qb_o, kb_o, vb_o, ka32_o, va32_o, kb32_o, vb32_o = outs
    h = _rms(x_ref[...], g_ref[...]).astype(BF16)
    r = lax.broadcasted_iota(jnp.int32, (MXU_DIM, MXU_DIM), 0) // DA
    c = lax.broadcasted_iota(jnp.int32, (MXU_DIM, MXU_DIM), 1) // DA
    ones_bd = (r == c).astype(BF16)
    cos2 = cos_ref[...]
    sin2 = sin_ref[...]

    def group(j):
        return _dot(h, w_ref[:, j * GROUP_W:(j + 1) * GROUP_W])

    def maybe_t(a):
        return a.T if transposed else a

    gqa = gqa_ref[...] * (DA ** -0.5 * LOG2E)
    gqb = gqb_ref[...] * (DB ** -0.5 * LOG2E)
    qa_o[...] = maybe_t(_head_norm(group(0), gqa, ones_bd)).astype(BF16)
    ka = _head_norm(group(1), gka_ref[...], ones_bd)
    va = group(2)
    va_t = maybe_t(va)
    ka_o[...] = ka.astype(BF16)
    va_o[...] = va_t.astype(BF16)
    qb = _rope(_head_norm(group(3), gqb, ones_bd), cos2, sin2)
    qb_o[...] = maybe_t(qb).astype(BF16)
    kb = _rope(_head_norm(group(4), gkb_ref[...], ones_bd), cos2, sin2)
    vb = group(5)
    kb_o[...] = kb.astype(BF16)
    vb_o[...] = maybe_t(vb).astype(BF16)
    _store_heads_interleaved(kb32_o, kb)
    _store_heads_interleaved(vb32_o, vb)

    @pl.when(pl.program_id(0) % tiles_per_seq == tiles_per_seq - 1)
    def _():
        ka32_o[...] = maybe_t(ka)
        va32_o[...] = va_t


def _proj(x, g, w_in, gqa, gka, gqb, gkb, cos2, sin2, seq, layer, depth, transposed, prev_b):
    n, d = x.shape
    keep = min(BAND_PREV, seq)
    if seq >= TOKEN_TILE:
        assert seq % TOKEN_TILE == 0 and keep == TOKEN_TILE
        tiles_per_seq = seq // TOKEN_TILE
    else:
        assert TOKEN_TILE % seq == 0 and keep == seq and not transposed
        tiles_per_seq = 1
    n_seq = n // seq
    pos_tiles = cos2.shape[0] // TOKEN_TILE
    row = pl.BlockSpec((TOKEN_TILE, d), lambda i: (i, 0))
    grp = pl.BlockSpec((TOKEN_TILE, GROUP_W), lambda i: (i, 0))
    grp_t = pl.BlockSpec((GROUP_W, TOKEN_TILE), lambda i: (0, i))
    tab = pl.BlockSpec((TOKEN_TILE, LANES), lambda i: (i % pos_tiles, 0))
    gain = _resident((1, GROUP_W))
    nat = jax.ShapeDtypeStruct((n, GROUP_W), BF16)
    tr = jax.ShapeDtypeStruct((GROUP_W, n), BF16)
    if transposed:
        qv_spec, qv_shape = grp_t, tr
        kept = pl.BlockSpec((None, GROUP_W, keep), lambda i: (i // tiles_per_seq, 0, 0))
        kept_shape = jax.ShapeDtypeStruct((n_seq, GROUP_W, keep), F32)
    else:
        qv_spec, qv_shape = grp, nat
        kept = pl.BlockSpec((TOKEN_TILE, GROUP_W), lambda i: (i // tiles_per_seq, 0))
        kept_shape = jax.ShapeDtypeStruct((n_seq * keep, GROUP_W), F32)
    cache_b = pl.BlockSpec((None, TOKEN_TILE * HB, EB), lambda i: (layer, i, 0))
    cache_b_shape = jax.ShapeDtypeStruct((depth, n * HB, EB), F32)
    aliased = prev_b is not None
    any_spec = pl.BlockSpec(memory_space=pl.ANY)
    return pl.pallas_call(
        functools.partial(_proj_kernel, tiles_per_seq=tiles_per_seq, transposed=transposed,
                          aliased=aliased),
        grid=(n // TOKEN_TILE,),
        in_specs=[row, _resident((1, d)), _resident(w_in.shape), gain, gain, gain, gain, tab, tab]
                 + ([any_spec, any_spec] if aliased else []),
        out_specs=[qv_spec, grp, qv_spec, qv_spec, grp, qv_spec, kept, kept, cache_b, cache_b],
        out_shape=[qv_shape, nat, qv_shape, qv_shape, nat, qv_shape, kept_shape, kept_shape,
                   cache_b_shape, cache_b_shape],
        input_output_aliases={9: 8, 10: 9} if aliased else {},
        compiler_params=_cparams("arbitrary"),
        name="proj",
    )(x, g, w_in, gqa, gka, gqb, gkb, cos2, sin2, *(prev_b if aliased else ()))


def _row_half_masks(cols=LANES):
    row = lax.broadcasted_iota(jnp.int32, (LANES, cols), 0)
    top = jnp.where(row < LANES // 2, 1.0, 0.0).astype(BF16)
    bottom = jnp.where(row >= LANES // 2, 1.0, 0.0).astype(BF16)
    return top, bottom


def _toeplitz(tab_c, tab_prev, rows, base_shift):
    lane = lax.broadcasted_iota(jnp.int32, (rows, LANES), 1)
    shift = lax.broadcasted_iota(jnp.int32, (rows, LANES), 0) + base_shift
    cur = pltpu.roll(jnp.broadcast_to(tab_c, (rows, LANES)), base_shift, 1, stride=1, stride_axis=0)
    prev = pltpu.roll(jnp.broadcast_to(tab_prev, (rows, LANES)), base_shift, 1, stride=1,
                      stride_axis=0)
    return jnp.where(lane >= shift, cur, prev)


def _band_prompt_kernel(qt_ref, k0_ref, k1_ref, k2_ref, v0_ref, v1_ref, v2_ref, rel_ref, o_ref,
                        bias_ref):
    i = pl.program_id(1)

    @pl.when((pl.program_id(0) == 0) & (i == 0))
    def _():
        kc = lax.broadcasted_iota(jnp.int32, (LANES, LANES), 0) // CHUNK
        qc = lax.broadcasted_iota(jnp.int32, (LANES, LANES), 1) // CHUNK
        for g in range(A_GROUPS):
            dc = kc + (LANES // CHUNK) * g - qc
            visible = (dc >= 0) & (dc <= PREV_CHUNKS)
            hi = (A_GROUPS - 1 - g) * LANES
            for hd in range(HA):
                t = _toeplitz(rel_ref[hd:hd + 1, hi + LANES:hi + 2 * LANES],
                              rel_ref[hd:hd + 1, hi:hi + LANES], LANES, 1)
                bias_ref[hd, g * LANES:(g + 1) * LANES, :] = jnp.where(visible, t * LOG2E, NEG)
        bias_ref[HA] = jnp.full((A_WIN, LANES), NEG, F32)

    top, bottom = _row_half_masks()
    k_refs = (k0_ref, k1_ref, k2_ref)
    v_refs = (v0_ref, v1_ref, v2_ref)
    halves = Q_TILE // LANES
    items = [(pair, half) for pair in range(HA // 2) for half in range(halves)]

    def scores(pair, half):
        ls = slice(pair * LANES, (pair + 1) * LANES)
        kwin = jnp.concatenate([r[:, ls] for r in k_refs], axis=0)[half * LANES:half * LANES + A_WIN]
        qt = qt_ref[ls, half * LANES:(half + 1) * LANES]
        return _dot(kwin, jnp.concatenate([qt * top, qt * bottom], axis=1))

    def finish(pair, half, st2):
        ls = slice(pair * LANES, (pair + 1) * LANES)
        ps, sums = [], []
        for sub in range(2):
            hd = 2 * pair + sub
            parts = []
            for g in range(A_GROUPS):
                idx = jnp.where(2 * i + half + g >= A_GROUPS - 1, hd, HA)
                parts.append(st2[g * LANES:(g + 1) * LANES, sub * LANES:(sub + 1) * LANES]
                             + bias_ref[idx, g * LANES:(g + 1) * LANES, :])
            s = jnp.concatenate(parts, axis=0)
            p = jnp.exp2(s - jnp.max(s, axis=0, keepdims=True))
            sums.append(jnp.sum(p, axis=0, keepdims=True))
            ps.append(p.astype(BF16))
        vwin = jnp.concatenate([r[ls, :] for r in v_refs], axis=1)[:, half * LANES:half * LANES + A_WIN]
        ot2 = _dot(vwin, jnp.concatenate(ps, axis=1))
        ot = jnp.concatenate([ot2[:DA, :LANES] * (1.0 / sums[0]),
                              ot2[DA:, LANES:] * (1.0 / sums[1])], axis=0)
        o_ref[half * LANES:(half + 1) * LANES, ls] = ot.T.astype(o_ref.dtype)

    pending = [scores(*item) for item in items[:A_LOOKAHEAD]]
    for n, item in enumerate(items):
        if n + A_LOOKAHEAD < len(items):
            pending.append(scores(*items[n + A_LOOKAHEAD]))
        finish(*item, pending.pop(0))


def _band_prompt(qt, k, vt, rel, bsz):
    w, n = qt.shape
    tiles = n // bsz // Q_TILE
    qspec = pl.BlockSpec((w, Q_TILE), lambda b, i: (0, b * tiles + i))
    ospec = pl.BlockSpec((Q_TILE, w), lambda b, i: (b * tiles + i, 0))

    def kspec(back):
        return pl.BlockSpec((Q_TILE, w), lambda b, i: (b * tiles + jnp.maximum(i - back, 0), 0))

    def vspec(back):
        return pl.BlockSpec((w, Q_TILE), lambda b, i: (0, b * tiles + jnp.maximum(i - back, 0)))

    return pl.pallas_call(
        _band_prompt_kernel,
        grid=(bsz, tiles),
        in_specs=[qspec, kspec(2), kspec(1), kspec(0), vspec(2), vspec(1), vspec(0),
                  _resident(rel.shape)],
        out_specs=ospec,
        out_shape=jax.ShapeDtypeStruct((n, w), BF16),
        scratch_shapes=[pltpu.VMEM((HA + 1, A_WIN, LANES), F32)],
        compiler_params=_cparams("arbitrary", "arbitrary"),
        name="band_prompt",
    )(qt, k, k, k, vt, vt, vt, rel)


def _pair_masks():
    lane = lax.broadcasted_iota(jnp.int32, (1, LANES), 1)
    lo = jnp.where(lane < DA, 1.0, 0.0).astype(BF16)
    hi = jnp.where(lane >= DA, 1.0, 0.0).astype(BF16)
    return lo, hi, lane < DA


def _softmax_pv(s_list, pv_list):
    m = None
    for s in s_list:
        mx = jnp.max(s, axis=-1, keepdims=True)
        m = mx if m is None else jnp.maximum(m, mx)
    l = None
    o = None
    for s, pv in zip(s_list, pv_list):
        p = jnp.exp2(s - m)
        ps = jnp.sum(p, axis=-1, keepdims=True)
        po = pv(p.astype(BF16))
        l = ps if l is None else l + ps
        o = po if o is None else o + po
    return o * (1.0 / l)


def _band_sample_kernel(q_ref, kct_ref, vct_ref, kn_ref, vn_ref, rel_ref, o_ref, bias_ref):
    t = q_ref.shape[1]

    @pl.when(pl.program_id(0) == 0)
    def _():
        for g in range(A_GROUPS):
            lo = (g + 1) * LANES
            for hd in range(HA):
                bias_ref[hd // 2, (hd % 2) * t:(hd % 2 + 1) * t, g * LANES:(g + 1) * LANES] = (
                    LOG2E * _toeplitz(rel_ref[hd:hd + 1, lo:lo + LANES],
                                      rel_ref[hd:hd + 1, lo - LANES:lo], t, 0))

    lo, hi, is_lo = _pair_masks()
    items = [(bi, pair) for bi in range(q_ref.shape[0]) for pair in range(HA // 2)]

    def scores(bi, pair):
        ls = slice(pair * LANES, (pair + 1) * LANES)
        q = q_ref[bi, :, ls]
        q2 = jnp.concatenate([q * lo, q * hi], axis=0)
        kct = kct_ref[bi, 2 * pair:2 * pair + 2].reshape(LANES, BAND_PREV).astype(BF16)
        return (_dot(q2, kct) + bias_ref[pair, :, :BAND_PREV],
                _dot_nt(q2, kn_ref[bi, :, ls]) + bias_ref[pair, :, BAND_PREV:BAND_PREV + t])

    def finish(bi, pair, sc, sn):
        ls = slice(pair * LANES, (pair + 1) * LANES)
        vct = vct_ref[bi, 2 * pair:2 * pair + 2].reshape(LANES, BAND_PREV).astype(BF16)
        o2 = _softmax_pv([sc, sn], [lambda p: _dot_nt(p, vct), lambda p: _dot(p, vn_ref[bi, :, ls])])
        o_ref[bi, :, ls] = jnp.where(is_lo, o2[:t], o2[t:]).astype(o_ref.dtype)

    pending = [scores(*item) for item in items[:A_LOOKAHEAD]]
    for n, item in enumerate(items):
        if n + A_LOOKAHEAD < len(items):
            pending.append(scores(*items[n + A_LOOKAHEAD]))
        finish(*item, *pending.pop(0))


def _band_sample(q, kn, vn, cache_kt, cache_vt, layer, rel, group):
    b, t, w = q.shape
    a_len = cache_kt.shape[4]
    assert b % group == 0 and a_len == BAND_PREV and t == CHUNK
    new = pl.BlockSpec((group, t, w), lambda i: (i, 0, 0))
    old = pl.BlockSpec((None, group, HA, DA, a_len), lambda i: (layer, i, 0, 0, 0))
    return pl.pallas_call(
        _band_sample_kernel,
        grid=(b // group,),
        in_specs=[new, old, old, new, new, _resident(rel.shape)],
        out_specs=new,
        out_shape=jax.ShapeDtypeStruct((b, t, w), BF16),
        scratch_shapes=[pltpu.VMEM((HA // 2, 2 * t, A_WIN), F32)],
        compiler_params=_cparams("arbitrary"),
        name="band_sample",
    )(q, cache_kt, cache_vt, kn, vn, rel)


def _lam(lq1_ref, lk1_ref, lq2_ref, lk2_ref, lam_init):
    s1 = jnp.sum(lq1_ref[...] * lk1_ref[...], axis=-1, keepdims=True)
    s2 = jnp.sum(lq2_ref[...] * lk2_ref[...], axis=-1, keepdims=True)
    return jnp.exp(s1) - jnp.exp(s2) + lam_init


def _diff_prompt_kernel(qt_ref, k_ref, vt_ref, lq1_ref, lk1_ref, lq2_ref, lk2_ref, gsub_ref,
                        o_ref, m_ref, l_ref, acc_ref, q2_ref, st_ref, diag_ref, *, lam_init):
    i = pl.program_id(1)
    lam = _lam(lq1_ref, lk1_ref, lq2_ref, lk2_ref, lam_init)
    top, bottom = _row_half_masks(Q_TILE)

    @pl.when((pl.program_id(0) == 0) & (i == 0))
    def _():
        kc = lax.broadcasted_iota(jnp.int32, (Q_TILE, 2 * Q_TILE), 0) // CHUNK
        qc = (lax.broadcasted_iota(jnp.int32, (Q_TILE, 2 * Q_TILE), 1) % Q_TILE) // CHUNK
        diag_ref[...] = jnp.where(kc <= qc, 0.0, NEG).astype(F32)

    def prep_q(hd):
        qt = qt_ref[hd * EB:(hd + 1) * EB, :]
        q2_ref[hd] = jnp.concatenate([qt * top, qt * bottom], axis=1)

    def key_rows(j):
        return pl.ds(pl.multiple_of(j * Q_TILE, Q_TILE), Q_TILE)

    def scores(j, hd):
        return _dot(k_ref[key_rows(j), hd * EB:(hd + 1) * EB], q2_ref[hd])

    def update(j, hd, st, diagonal):
        if diagonal:
            st = st + diag_ref[...]
        m_old = m_ref[hd]
        m_new = jnp.maximum(m_old, jnp.max(st, axis=0, keepdims=True))
        p = jnp.exp2(st - m_new)
        alpha = jnp.exp2(m_old - m_new)
        m_ref[hd] = m_new
        l_ref[hd] = alpha * l_ref[hd] + jnp.sum(p, axis=0, keepdims=True)
        pv = _dot(vt_ref[hd * EB:(hd + 1) * EB, key_rows(j)], p.astype(BF16))
        acc_ref[hd] = alpha * acc_ref[hd] + pv

    for n in range(B_LOOKAHEAD):
        prep_q(n)
        st_ref[n] = scores(0, n)
    for n in range(B_LOOKAHEAD, HB):
        prep_q(n)
    m_ref[...] = jnp.full(m_ref.shape, NEG, F32)
    l_ref[...] = jnp.zeros(l_ref.shape, F32)
    acc_ref[...] = jnp.zeros(acc_ref.shape, F32)

    def body(j, carry):
        pending = [st_ref[n] for n in range(B_LOOKAHEAD)]
        for hd in range(HB):
            ahead = hd + B_LOOKAHEAD
            pending.append(scores(j, ahead) if ahead < HB else scores(j + 1, ahead - HB))
            update(j, hd, pending.pop(0), False)
        for n in range(B_LOOKAHEAD):
            st_ref[n] = pending[n]
        return carry

    lax.fori_loop(0, i, body, 0)
    pending = [st_ref[n] for n in range(B_LOOKAHEAD)]
    for hd in range(HB):
        if hd + B_LOOKAHEAD < HB:
            pending.append(scores(i, hd + B_LOOKAHEAD))
        update(i, hd, pending.pop(0), True)
        o12 = acc_ref[hd] * (1.0 / l_ref[hd])
        o = o12[:, :Q_TILE] - lam * o12[:, Q_TILE:]
        o = o * lax.rsqrt(jnp.mean(o * o, axis=0, keepdims=True) + EPS) * gsub_ref[...]
        o_ref[:, hd * EB:(hd + 1) * EB] = (o * (1.0 - lam_init)).T.astype(o_ref.dtype)


def _diff_prompt(qt, k, vt, lams, gsub_col, lam_init, bsz):
    w, n = qt.shape
    s = n // bsz
    tiles = s // Q_TILE
    qspec = pl.BlockSpec((w, Q_TILE), lambda b, i: (0, b * tiles + i))
    ospec = pl.BlockSpec((Q_TILE, w), lambda b, i: (b * tiles + i, 0))
    kfull = pl.BlockSpec((s, w), lambda b, i: (b, 0))
    vfull = pl.BlockSpec((w, s), lambda b, i: (0, b))
    vec = _resident((1, DB))
    return pl.pallas_call(
        functools.partial(_diff_prompt_kernel, lam_init=lam_init),
        grid=(bsz, tiles),
        in_specs=[qspec, kfull, vfull, vec, vec, vec, vec, _resident((EB, 1))],
        out_specs=ospec,
        out_shape=jax.ShapeDtypeStruct((n, w), BF16),
        scratch_shapes=[pltpu.VMEM((HB, 1, 2 * Q_TILE), F32), pltpu.VMEM((HB, 1, 2 * Q_TILE), F32),
                        pltpu.VMEM((HB, EB, 2 * Q_TILE), F32),
                        pltpu.VMEM((HB, EB, 2 * Q_TILE), BF16),
                        pltpu.VMEM((B_LOOKAHEAD, Q_TILE, 2 * Q_TILE), F32),
                        pltpu.VMEM((Q_TILE, 2 * Q_TILE), F32)],
        compiler_params=_cparams("arbitrary", "arbitrary"),
        name="diff_prompt",
    )(qt, k, vt, *lams, gsub_col)


def _diff_sample_kernel(q_ref, kc_ref, vc_ref, kn_ref, vn_ref, lq1_ref, lk1_ref, lq2_ref,
                        lk2_ref, gsub_ref, o_ref, *, lam_init):
    lam = _lam(lq1_ref, lk1_ref, lq2_ref, lk2_ref, lam_init)
    lo, hi, _ = _pair_masks()
    past = kc_ref.shape[0] // HB
    t = q_ref.shape[1]

    def scores(hd):
        ls = slice(hd * EB, (hd + 1) * EB)
        q = q_ref[0, :, ls]
        q12 = jnp.concatenate([q * lo, q * hi], axis=0)
        kc = kc_ref[pl.ds(hd, past, stride=HB), :].astype(BF16)
        return _dot_nt(q12, kc), _dot_nt(q12, kn_ref[0, :, ls])

    def finish(hd, sc, sn):
        ls = slice(hd * EB, (hd + 1) * EB)
        vc = vc_ref[pl.ds(hd, past, stride=HB), :].astype(BF16)
        o12 = _softmax_pv([sc, sn], [lambda p: _dot(p, vc), lambda p: _dot(p, vn_ref[0, :, ls])])
        o = o12[:t] - lam * o12[t:]
        o_ref[0, :, ls] = (_rms(o, gsub_ref[...]) * (1.0 - lam_init)).astype(o_ref.dtype)

    pending = [scores(hd) for hd in range(B_LOOKAHEAD)]
    for hd in range(HB):
        if hd + B_LOOKAHEAD < HB:
            pending.append(scores(hd + B_LOOKAHEAD))
        finish(hd, *pending.pop(0))


def _diff_sample(q, kn, vn, cache_k, cache_v, layer, lams, gsub, lam_init):
    b, t, w = q.shape
    rows = cache_k.shape[2]
    new = pl.BlockSpec((1, t, w), lambda i: (i, 0, 0))
    old = pl.BlockSpec((None, None, rows, EB), lambda i: (layer, i, 0, 0))
    vec = _resident((1, DB))
    return pl.pallas_call(
        functools.partial(_diff_sample_kernel, lam_init=lam_init),
        grid=(b,),
        in_specs=[new, old, old, new, new, vec, vec, vec, vec, _resident((1, EB))],
        out_specs=new,
        out_shape=jax.ShapeDtypeStruct((b, t, w), BF16),
        compiler_params=_cparams("parallel"),
        name="diff_sample",
    )(q, cache_k, cache_v, kn, vn, *lams, gsub)


def _rope_tables(pos):
    half = DB // 2
    inv = ROPE_THETA ** (-jnp.arange(half, dtype=F32) * 2.0 / DB)
    ang = pos.astype(F32)[:, None] * inv[None, :]
    cos, sin = jnp.cos(ang), jnp.sin(ang)
    cos2 = jnp.concatenate([cos, cos], axis=-1)
    sin2 = jnp.concatenate([-sin, sin], axis=-1)
    return jnp.tile(cos2, (1, LANES // DB)), jnp.tile(sin2, (1, LANES // DB))


def _rel_rows(table):
    table = table.astype(F32)
    n_const = REL_SPAN - (2 * MAX_REL - 1)
    const = jnp.broadcast_to(table[:, 2 * MAX_REL:], (HA, n_const))
    desc = jnp.concatenate([table[:, 1:2 * MAX_REL], const], axis=1)
    asc = jnp.concatenate([const, table[:, 1:2 * MAX_REL][:, ::-1]], axis=1)
    return asc, desc


def kernel(x_prompt, x_sample, cache_a_k, cache_a_v, cache_b_k, cache_b_v,
           g_ffn1, w1_gate, w1_up, w1_down, g_mix, w_in, g_qa, g_ka, g_qb, g_kb,
           rel_bias, lam_q1, lam_k1, lam_q2, lam_k2, g_sub, w_out,
           g_ffn2, w2_gate, w2_up, w2_down):
    bp, s, d = x_prompt.shape
    bd, t, _ = x_sample.shape
    depth = w_in.shape[0]
    past = cache_b_k.shape[2]
    a_len = cache_a_k.shape[2]
    assert a_len == BAND_PREV and t == CHUNK and s % Q_TILE == 0 and MAX_REL == LANES
    keep_p = min(BAND_PREV, s)

    cos_p, sin_p = _rope_tables(jnp.arange(s))
    reps = TOKEN_TILE // t
    cos_s, sin_s = (jnp.tile(a, (reps, 1)) for a in _rope_tables(past + jnp.arange(t)))

    cakt = jnp.transpose(cache_a_k, (0, 1, 3, 4, 2))
    cavt = jnp.transpose(cache_a_v, (0, 1, 3, 4, 2))
    cbk = cache_b_k.reshape(depth, bd, past * HB, EB)
    cbv = cache_b_v.reshape(depth, bd, past * HB, EB)

    yp = x_prompt.reshape(bp * s, d)
    ys = x_sample.reshape(bd * t, d)
    pak, pav, sak, sav = [], [], [], []
    pb = sb = None
    for l in range(depth):
        lam_init = 0.8 - 0.6 * math.exp(-0.3 * l)
        row = lambda a: a[l].astype(F32).reshape(1, -1)
        tiled = lambda a: jnp.tile(row(a), (1, GROUP_W // a.shape[1]))
        w1 = (w1_gate[l].astype(BF16), w1_up[l].astype(BF16), w1_down[l].astype(BF16))
        w2 = (w2_gate[l].astype(BF16), w2_up[l].astype(BF16), w2_down[l].astype(BF16))
        wi = w_in[l].astype(BF16)
        woa = w_out[l, :GROUP_W].astype(BF16)
        wob = w_out[l, GROUP_W:].astype(BF16)
        gains = (tiled(g_qa), tiled(g_ka), tiled(g_qb), tiled(g_kb))
        lams = (row(lam_q1), row(lam_k1), row(lam_q2), row(lam_k2))
        gsub = row(g_sub)
        rel_asc, rel_desc = _rel_rows(rel_bias[l])

        yp = _ffn(yp, row(g_ffn1), *w1)
        qat, ka, vat, qbt, kb, vbt, kat32, vat32, *pb = _proj(
            yp, row(g_mix), wi, *gains, cos_p, sin_p, s, l, depth, True, pb)
        oa = _band_prompt(qat, ka, vat, rel_desc, bp)
        ob = _diff_prompt(qbt, kb, vbt, lams, gsub.reshape(EB, 1), lam_init, bp)
        yp = _out_ffn(yp, oa, ob, woa, wob, row(g_ffn2), *w2)
        pak.append(kat32)
        pav.append(vat32)

        ys = _ffn(ys, row(g_ffn1), *w1)
        qa, ka, va, qb, kb, vb, ka32, va32, *sb = _proj(
            ys, row(g_mix), wi, *gains, cos_s, sin_s, t, l, depth, False, sb)
        sh = lambda a: a.reshape(bd, t, GROUP_W)
        oa = _band_sample(sh(qa), sh(ka), sh(va), cakt, cavt, l, rel_asc, 4)
        ob = _diff_sample(sh(qb), sh(kb), sh(vb), cbk, cbv, l, lams, gsub, lam_init)
        ys = _out_ffn(ys, oa.reshape(bd * t, GROUP_W), ob.reshape(bd * t, GROUP_W),
                      woa, wob, row(g_ffn2), *w2)
        sak.append(ka32)
        sav.append(va32)

    def kept_prompt(parts):
        a = jnp.stack(parts).reshape(depth, bp, HA, DA, keep_p)
        return jnp.transpose(a, (0, 1, 4, 2, 3))

    return (yp.reshape(bp, s, d), ys.reshape(bd, t, d),
            kept_prompt(pak), kept_prompt(pav),
            pb[0].reshape(depth, bp, s, HB, EB), pb[1].reshape(depth, bp, s, HB, EB),
            jnp.stack(sak).reshape(depth, bd, t, HA, DA),
            jnp.stack(sav).reshape(depth, bd, t, HA, DA),
            sb[0].reshape(depth, bd, t, HB, EB), sb[1].reshape(depth, bd, t, HB, EB))
```

```python
import functools
import math

import jax
import jax.numpy as jnp
from jax import lax
from jax.experimental import pallas as pl
from jax.experimental.pallas import tpu as pltpu

CHUNK = 64
PREV_CHUNKS = 8
BAND_PREV = PREV_CHUNKS * CHUNK
MAX_REL = 128
HA = 8
DA = 64
HB = 4
DB = 64
EB = 2 * DB
GROUP_W = HA * DA
ROPE_THETA = 10000.0
EPS = 1e-6
NEG = -1e30
LOG2E = math.log2(math.e)

LANES = 128
MXU_DIM = 256
TOKEN_TILE = 512
Q_TILE = 256
A_WIN = BAND_PREV + LANES
A_GROUPS = A_WIN // LANES
REL_SPAN = A_WIN + LANES
A_LOOKAHEAD = 3
B_LOOKAHEAD = 2
VMEM_LIMIT = 56 * 1024 * 1024

F32 = jnp.float32
BF16 = jnp.bfloat16


def _cparams(*sem):
    return pltpu.CompilerParams(dimension_semantics=sem, vmem_limit_bytes=VMEM_LIMIT)


def _resident(shape):
    nd = len(shape)
    return pl.BlockSpec(shape, lambda *_: (0,) * nd, pipeline_mode=pl.Buffered(1))


def _dot(a, b):
    return jnp.dot(a, b, preferred_element_type=F32)


def _dot_nt(a, b):
    return lax.dot_general(a, b, (((1,), (1,)), ((), ())), preferred_element_type=F32)


def _rms(x, g):
    return x * lax.rsqrt(jnp.mean(x * x, axis=-1, keepdims=True) + EPS) * g


def _swiglu_half(x, g, wg_ref, wu_ref, wd_ref):
    h = _rms(x, g).astype(BF16)
    d_ff = wg_ref.shape[1]
    split = min(d_ff, pl.cdiv(d_ff, 2 * MXU_DIM) * MXU_DIM)
    y = None
    for lo, hi in ((0, split), (split, d_ff)):
        if lo == hi:
            continue
        gate = _dot(h, wg_ref[:, lo:hi])
        up = _dot(h, wu_ref[:, lo:hi])
        a = (gate / (1.0 + jnp.exp(-gate)) * up).astype(BF16)
        part = _dot(a, wd_ref[lo:hi, :])
        y = part if y is None else y + part
    return x + 0.5 * y


def _ffn_kernel(x_ref, g_ref, wg_ref, wu_ref, wd_ref, o_ref):
    o_ref[...] = _swiglu_half(x_ref[...], g_ref[...], wg_ref, wu_ref, wd_ref)


def _out_ffn_kernel(x_ref, oa_ref, ob_ref, woa_ref, wob_ref, g_ref, wg_ref, wu_ref, wd_ref,
                    o_ref):
    x = x_ref[...] + _dot(oa_ref[...], woa_ref[...]) + _dot(ob_ref[...], wob_ref[...])
    o_ref[...] = _swiglu_half(x, g_ref[...], wg_ref, wu_ref, wd_ref)


def _ffn(x, g, wg, wu, wd):
    n, d = x.shape
    d_ff = wg.shape[1]
    assert n % TOKEN_TILE == 0
    row = pl.BlockSpec((TOKEN_TILE, d), lambda i: (i, 0))
    return pl.pallas_call(
        _ffn_kernel,
        grid=(n // TOKEN_TILE,),
        in_specs=[row, _resident((1, d)), _resident((d, d_ff)), _resident((d, d_ff)),
                  _resident((d_ff, d))],
        out_specs=row,
        out_shape=jax.ShapeDtypeStruct((n, d), F32),
        compiler_params=_cparams("parallel"),
        name="ffn",
    )(x, g, wg, wu, wd)


def _out_ffn(x, oa, ob, woa, wob, g, wg, wu, wd):
    n, d = x.shape
    d_ff = wg.shape[1]
    row = pl.BlockSpec((TOKEN_TILE, d), lambda i: (i, 0))
    grp = pl.BlockSpec((TOKEN_TILE, GROUP_W), lambda i: (i, 0))
    return pl.pallas_call(
        _out_ffn_kernel,
        grid=(n // TOKEN_TILE,),
        in_specs=[row, grp, grp, _resident((GROUP_W, d)), _resident((GROUP_W, d)),
                  _resident((1, d)), _resident((d, d_ff)), _resident((d, d_ff)),
                  _resident((d_ff, d))],
        out_specs=row,
        out_shape=jax.ShapeDtypeStruct((n, d), F32),
        compiler_params=_cparams("parallel"),
        name="out_ffn",
    )(x, oa, ob, woa, wob, g, wg, wu, wd)


def _head_norm(z, gain, ones_bd):
    outs = []
    for c in range(0, z.shape[1], MXU_DIM):
        zc = z[:, c:c + MXU_DIM]
        ss = _dot((zc * zc).astype(BF16), ones_bd)
        outs.append(zc * lax.rsqrt(ss * (1.0 / DA) + EPS) * gain[:, c:c + MXU_DIM])
    return jnp.concatenate(outs, axis=1)


def _rope(y, cos2, sin2):
    first_half = (lax.broadcasted_iota(jnp.int32, (1, LANES), 1) % DB) < (DB // 2)
    outs = []
    for c in range(0, y.shape[1], LANES):
        yc = y[:, c:c + LANES]
        swapped = jnp.where(first_half, pltpu.roll(yc, LANES - DB // 2, 1),
                            pltpu.roll(yc, DB // 2, 1))
        outs.append(yc * cos2 + swapped * sin2)
    return jnp.concatenate(outs, axis=1)


def _store_heads_interleaved(o_ref, slot, val):
    m = val.shape[0]
    for s in range(o_ref.shape[0]):
        if s == slot:
            for hd in range(HB):
                o_ref[s, pl.ds(hd, m, stride=HB), :] = val[:, hd * EB:(hd + 1) * EB]
        else:
            o_ref[s] = jnp.zeros(o_ref.shape[1:], o_ref.dtype)


def _proj_kernel(*refs, tiles_per_seq, transposed, aliased, slot):
    (x_ref, g_ref, w_ref, gqa_ref, gka_ref, gqb_ref, gkb_ref, cos_ref, sin_ref) = refs[:9]
    outs = refs[9 + (2 if aliased else 0):]
    qa_o, ka_o, va_o, qb_o, kb_o, vb_o, ka32_o, va32_o, kb32_o, vb32_o = outs
    h = _rms(x_ref[...], g_ref[...]).astype(BF16)
    r = lax.broadcasted_iota(jnp.int32, (MXU_DIM, MXU_DIM), 0) // DA
    c = lax.broadcasted_iota(jnp.int32, (MXU_DIM, MXU_DIM), 1) // DA
    ones_bd = (r == c).astype(BF16)
    cos2 = cos_ref[...]
    sin2 = sin_ref[...]

    def group(j):
        return _dot(h, w_ref[:, j * GROUP_W:(j + 1) * GROUP_W])

    def maybe_t(a):
        return a.T if transposed else a

    gqa = gqa_ref[...] * (DA ** -0.5 * LOG2E)
    gqb = gqb_ref[...] * (DB ** -0.5 * LOG2E)
    kept = []

    def finish_qa(z):
        qa_o[...] = maybe_t(_head_norm(z, gqa, ones_bd)).astype(BF16)

    def finish_ka(z):
        ka = _head_norm(z, gka_ref[...], ones_bd)
        ka_o[...] = ka.astype(BF16)
        kept.append((ka32_o, maybe_t(ka)))

    def finish_va(z):
        va_t = maybe_t(z)
        va_o[...] = va_t.astype(BF16)
        kept.append((va32_o, va_t))

    def finish_qb(z):
        qb_o[...] = maybe_t(_rope(_head_norm(z, gqb, ones_bd), cos2, sin2)).astype(BF16)

    def finish_kb(z):
        kb = _rope(_head_norm(z, gkb_ref[...], ones_bd), cos2, sin2)
        kb_o[...] = kb.astype(BF16)
        _store_heads_interleaved(kb32_o, slot, kb)

    def finish_vb(z):
        vb_o[...] = maybe_t(z).astype(BF16)
        _store_heads_interleaved(vb32_o, slot, z)

    order = ((3, finish_qb), (4, finish_kb), (0, finish_qa), (1, finish_ka), (5, finish_vb),
             (2, finish_va))
    z = group(order[0][0])
    for n, (_, finish) in enumerate(order):
        nxt = group(order[n + 1][0]) if n + 1 < len(order) else None
        finish(z)
        z = nxt

    @pl.when(pl.program_id(0) % tiles_per_seq == tiles_per_seq - 1)
    def _():
        for o_ref, val in kept:
            o_ref[...] = val


def _proj(x, g, w_in, gqa, gka, gqb, gkb, cos2, sin2, seq, layer, depth, transposed, prev_b):
    n, d = x.shape
    keep = min(BAND_PREV, seq)
    if seq >= TOKEN_TILE:
        assert seq % TOKEN_TILE == 0 and keep == TOKEN_TILE
        tiles_per_seq = seq // TOKEN_TILE
    else:
        assert TOKEN_TILE % seq == 0 and keep == seq and not transposed
        tiles_per_seq = 1
    n_seq = n // seq
    pos_tiles = cos2.shape[0] // TOKEN_TILE
    row = pl.BlockSpec((TOKEN_TILE, d), lambda i: (i, 0))
    grp = pl.BlockSpec((TOKEN_TILE, GROUP_W), lambda i: (i, 0))
    grp_t = pl.BlockSpec((GROUP_W, TOKEN_TILE), lambda i: (0, i))
    tab = pl.BlockSpec((TOKEN_TILE, LANES), lambda i: (i % pos_tiles, 0))
    gain = _resident((1, GROUP_W))
    nat = jax.ShapeDtypeStruct((n, GROUP_W), BF16)
    tr = jax.ShapeDtypeStruct((GROUP_W, n), BF16)
    if transposed:
        qv_spec, qv_shape = grp_t, tr
        kept = pl.BlockSpec((None, GROUP_W, keep), lambda i: (i // tiles_per_seq, 0, 0))
        kept_shape = jax.ShapeDtypeStruct((n_seq, GROUP_W, keep), F32)
    else:
        qv_spec, qv_shape = grp, nat
        kept = pl.BlockSpec((TOKEN_TILE, GROUP_W), lambda i: (i // tiles_per_seq, 0))
        kept_shape = jax.ShapeDtypeStruct((n_seq * keep, GROUP_W), F32)
    aliased = prev_b is not None
    if aliased:
        cache_b = pl.BlockSpec((1, TOKEN_TILE * HB, EB), lambda i: (layer, i, 0))
        slot = 0
    else:
        cache_b = pl.BlockSpec((depth, TOKEN_TILE * HB, EB), lambda i: (0, i, 0))
        slot = layer
    cache_b_shape = jax.ShapeDtypeStruct((depth, n * HB, EB), F32)
    any_spec = pl.BlockSpec(memory_space=pl.ANY)
    return pl.pallas_call(
        functools.partial(_proj_kernel, tiles_per_seq=tiles_per_seq, transposed=transposed,
                          aliased=aliased, slot=slot),
        grid=(n // TOKEN_TILE,),
        in_specs=[row, _resident((1, d)), _resident(w_in.shape), gain, gain, gain, gain, tab, tab]
                 + ([any_spec, any_spec] if aliased else []),
        out_specs=[qv_spec, grp, qv_spec, qv_spec, grp, qv_spec, kept, kept, cache_b, cache_b],
        out_shape=[qv_shape, nat, qv_shape, qv_shape, nat, qv_shape, kept_shape, kept_shape,
                   cache_b_shape, cache_b_shape],
        input_output_aliases={9: 8, 10: 9} if aliased else {},
        compiler_params=_cparams("arbitrary"),
        name="proj",
    )(x, g, w_in, gqa, gka, gqb, gkb, cos2, sin2, *(prev_b if aliased else ()))


def _row_half_masks(cols=LANES):
    row = lax.broadcasted_iota(jnp.int32, (LANES, cols), 0)
    top = jnp.where(row < LANES // 2, 1.0, 0.0).astype(BF16)
    bottom = jnp.where(row >= LANES // 2, 1.0, 0.0).astype(BF16)
    return top, bottom


def _toeplitz(tab_c, tab_prev, rows, base_shift):
    lane = lax.broadcasted_iota(jnp.int32, (rows, LANES), 1)
    shift = lax.broadcasted_iota(jnp.int32, (rows, LANES), 0) + base_shift
    cur = pltpu.roll(jnp.broadcast_to(tab_c, (rows, LANES)), base_shift, 1, stride=1, stride_axis=0)
    prev = pltpu.roll(jnp.broadcast_to(tab_prev, (rows, LANES)), base_shift, 1, stride=1,
                      stride_axis=0)
    return jnp.where(lane >= shift, cur, prev)


def _band_prompt_kernel(qt_ref, k0_ref, k1_ref, k2_ref, v0_ref, v1_ref, v2_ref, rel_ref, o_ref,
                        bias_ref):
    i = pl.program_id(1)

    @pl.when((pl.program_id(0) == 0) & (i == 0))
    def _():
        kc = lax.broadcasted_iota(jnp.int32, (LANES, LANES), 0) // CHUNK
        qc = lax.broadcasted_iota(jnp.int32, (LANES, LANES), 1) // CHUNK
        for g in range(A_GROUPS):
            dc = kc + (LANES // CHUNK) * g - qc
            visible = (dc >= 0) & (dc <= PREV_CHUNKS)
            hi = (A_GROUPS - 1 - g) * LANES
            for hd in range(HA):
                t = _toeplitz(rel_ref[hd:hd + 1, hi + LANES:hi + 2 * LANES],
                              rel_ref[hd:hd + 1, hi:hi + LANES], LANES, 1)
                bias_ref[hd, g * LANES:(g + 1) * LANES, :] = jnp.where(visible, t * LOG2E, NEG)
        bias_ref[HA] = jnp.full((A_WIN, LANES), NEG, F32)

    top, bottom = _row_half_masks()
    k_refs = (k0_ref, k1_ref, k2_ref)
    v_refs = (v0_ref, v1_ref, v2_ref)
    halves = Q_TILE // LANES
    items = [(pair, half) for pair in range(HA // 2) for half in range(halves)]

    def scores(pair, half):
        ls = slice(pair * LANES, (pair + 1) * LANES)
        kwin = jnp.concatenate([r[:, ls] for r in k_refs], axis=0)[half * LANES:half * LANES + A_WIN]
        qt = qt_ref[ls, half * LANES:(half + 1) * LANES]
        return _dot(kwin, jnp.concatenate([qt * top, qt * bottom], axis=1))

    def finish(pair, half, st2):
        ls = slice(pair * LANES, (pair + 1) * LANES)
        ps, sums = [], []
        for sub in range(2):
            hd = 2 * pair + sub
            parts = []
            for g in range(A_GROUPS):
                idx = jnp.where(2 * i + half + g >= A_GROUPS - 1, hd, HA)
                parts.append(st2[g * LANES:(g + 1) * LANES, sub * LANES:(sub + 1) * LANES]
                             + bias_ref[idx, g * LANES:(g + 1) * LANES, :])
            s = jnp.concatenate(parts, axis=0)
            p = jnp.exp2(s - jnp.max(s, axis=0, keepdims=True))
            sums.append(jnp.sum(p, axis=0, keepdims=True))
            ps.append(p.astype(BF16))
        vwin = jnp.concatenate([r[ls, :] for r in v_refs], axis=1)[:, half * LANES:half * LANES + A_WIN]
        ot2 = _dot(vwin, jnp.concatenate(ps, axis=1))
        ot = jnp.concatenate([ot2[:DA, :LANES] * (1.0 / sums[0]),
                              ot2[DA:, LANES:] * (1.0 / sums[1])], axis=0)
        o_ref[half * LANES:(half + 1) * LANES, ls] = ot.T.astype(o_ref.dtype)

    pending = [scores(*item) for item in items[:A_LOOKAHEAD]]
    for n, item in enumerate(items):
        if n + A_LOOKAHEAD < len(items):
            pending.append(scores(*items[n + A_LOOKAHEAD]))
        finish(*item, pending.pop(0))


def _band_prompt(qt, k, vt, rel, bsz):
    w, n = qt.shape
    tiles = n // bsz // Q_TILE
    qspec = pl.BlockSpec((w, Q_TILE), lambda b, i: (0, b * tiles + i))
    ospec = pl.BlockSpec((Q_TILE, w), lambda b, i: (b * tiles + i, 0))

    def kspec(back):
        return pl.BlockSpec((Q_TILE, w), lambda b, i: (b * tiles + jnp.maximum(i - back, 0), 0))

    def vspec(back):
        return pl.BlockSpec((w, Q_TILE), lambda b, i: (0, b * tiles + jnp.maximum(i - back, 0)))

    return pl.pallas_call(
        _band_prompt_kernel,
        grid=(bsz, tiles),
        in_specs=[qspec, kspec(2), kspec(1), kspec(0), vspec(2), vspec(1), vspec(0),
                  _resident(rel.shape)],
        out_specs=ospec,
        out_shape=jax.ShapeDtypeStruct((n, w), BF16),
        scratch_shapes=[pltpu.VMEM((HA + 1, A_WIN, LANES), F32)],
        compiler_params=_cparams("arbitrary", "arbitrary"),
        name="band_prompt",
    )(qt, k, k, k, vt, vt, vt, rel)


def _pair_masks():
    lane = lax.broadcasted_iota(jnp.int32, (1, LANES), 1)
    lo = jnp.where(lane < DA, 1.0, 0.0).astype(BF16)
    hi = jnp.where(lane >= DA, 1.0, 0.0).astype(BF16)
    return lo, hi, lane < DA


def _softmax_pv(s_list, pv_list):
    m = None
    for s in s_list:
        mx = jnp.max(s, axis=-1, keepdims=True)
        m = mx if m is None else jnp.maximum(m, mx)
    l = None
    o = None
    for s, pv in zip(s_list, pv_list):
        p = jnp.exp2(s - m)
        ps = jnp.sum(p, axis=-1, keepdims=True)
        po = pv(p.astype(BF16))
        l = ps if l is None else l + ps
        o = po if o is None else o + po
    return o * (1.0 / l)


def _band_sample_kernel(q_ref, kct_ref, vct_ref, kn_ref, vn_ref, rel_ref, o_ref, bias_ref):
    t = q_ref.shape[1]

    @pl.when(pl.program_id(0) == 0)
    def _():
        for g in range(A_GROUPS):
            lo = (g + 1) * LANES
            for hd in range(HA):
                bias_ref[hd // 2, (hd % 2) * t:(hd % 2 + 1) * t, g * LANES:(g + 1) * LANES] = (
                    LOG2E * _toeplitz(rel_ref[hd:hd + 1, lo:lo + LANES],
                                      rel_ref[hd:hd + 1, lo - LANES:lo], t, 0))

    lo, hi, is_lo = _pair_masks()
    items = [(bi, pair) for bi in range(q_ref.shape[0]) for pair in range(HA // 2)]

    def scores(bi, pair):
        ls = slice(pair * LANES, (pair + 1) * LANES)
        q = q_ref[bi, :, ls]
        q2 = jnp.concatenate([q * lo, q * hi], axis=0)
        kct = kct_ref[bi, 2 * pair:2 * pair + 2].reshape(LANES, BAND_PREV).astype(BF16)
        return (_dot(q2, kct) + bias_ref[pair, :, :BAND_PREV],
                _dot_nt(q2, kn_ref[bi, :, ls]) + bias_ref[pair, :, BAND_PREV:BAND_PREV + t])

    def finish(bi, pair, sc, sn):
        ls = slice(pair * LANES, (pair + 1) * LANES)
        vct = vct_ref[bi, 2 * pair:2 * pair + 2].reshape(LANES, BAND_PREV).astype(BF16)
        o2 = _softmax_pv([sc, sn], [lambda p: _dot_nt(p, vct), lambda p: _dot(p, vn_ref[bi, :, ls])])
        o_ref[bi, :, ls] = jnp.where(is_lo, o2[:t], o2[t:]).astype(o_ref.dtype)

    pending = [scores(*item) for item in items[:A_LOOKAHEAD]]
    for n, item in enumerate(items):
        if n + A_LOOKAHEAD < len(items):
            pending.append(scores(*items[n + A_LOOKAHEAD]))
        finish(*item, *pending.pop(0))


def _band_sample(q, kn, vn, cache_kt, cache_vt, layer, rel, group):
    b, t, w = q.shape
    a_len = cache_kt.shape[4]
    assert b % group == 0 and a_len == BAND_PREV and t == CHUNK
    new = pl.BlockSpec((group, t, w), lambda i: (i, 0, 0))
    old = pl.BlockSpec((None, group, HA, DA, a_len), lambda i: (layer, i, 0, 0, 0))
    return pl.pallas_call(
        _band_sample_kernel,
        grid=(b // group,),
        in_specs=[new, old, old, new, new, _resident(rel.shape)],
        out_specs=new,
        out_shape=jax.ShapeDtypeStruct((b, t, w), BF16),
        scratch_shapes=[pltpu.VMEM((HA // 2, 2 * t, A_WIN), F32)],
        compiler_params=_cparams("arbitrary"),
        name="band_sample",
    )(q, cache_kt, cache_vt, kn, vn, rel)


def _lam(lq1_ref, lk1_ref, lq2_ref, lk2_ref, lam_init):
    s1 = jnp.sum(lq1_ref[...] * lk1_ref[...], axis=-1, keepdims=True)
    s2 = jnp.sum(lq2_ref[...] * lk2_ref[...], axis=-1, keepdims=True)
    return jnp.exp(s1) - jnp.exp(s2) + lam_init


def _diff_prompt_kernel(qt_ref, k_ref, vt_ref, lq1_ref, lk1_ref, lq2_ref, lk2_ref, gsub_ref,
                        o_ref, m_ref, l_ref, acc_ref, q2_ref, diag_ref, *, lam_init, tiles):
    i = pl.program_id(1)
    lam = _lam(lq1_ref, lk1_ref, lq2_ref, lk2_ref, lam_init)
    top, bottom = _row_half_masks(Q_TILE)

    @pl.when((pl.program_id(0) == 0) & (i == 0))
    def _():
        kc = lax.broadcasted_iota(jnp.int32, (Q_TILE, 2 * Q_TILE), 0) // CHUNK
        qc = (lax.broadcasted_iota(jnp.int32, (Q_TILE, 2 * Q_TILE), 1) % Q_TILE) // CHUNK
        diag_ref[...] = jnp.where(kc <= qc, 0.0, NEG).astype(F32)

    def prep_q(hd):
        qt = qt_ref[hd * EB:(hd + 1) * EB, :]
        q2_ref[hd] = jnp.concatenate([qt * top, qt * bottom], axis=1)

    def scores(j, hd):
        return _dot(k_ref[j * Q_TILE:(j + 1) * Q_TILE, hd * EB:(hd + 1) * EB], q2_ref[hd])

    def update(j, hd, st, first, diagonal):
        if diagonal:
            st = st + diag_ref[...]
        vt = vt_ref[hd * EB:(hd + 1) * EB, j * Q_TILE:(j + 1) * Q_TILE]
        m_new = jnp.max(st, axis=0, keepdims=True)
        if first:
            p = jnp.exp2(st - m_new)
            l_ref[hd] = jnp.sum(p, axis=0, keepdims=True)
            acc_ref[hd] = _dot(vt, p.astype(BF16))
        else:
            m_old = m_ref[hd]
            m_new = jnp.maximum(m_old, m_new)
            p = jnp.exp2(st - m_new)
            alpha = jnp.exp2(m_old - m_new)
            l_ref[hd] = alpha * l_ref[hd] + jnp.sum(p, axis=0, keepdims=True)
            acc_ref[hd] = alpha * acc_ref[hd] + _dot(vt, p.astype(BF16))
        m_ref[hd] = m_new

    def finish(hd):
        o12 = acc_ref[hd] * (1.0 / l_ref[hd])
        o = o12[:, :Q_TILE] - lam * o12[:, Q_TILE:]
        o = o * lax.rsqrt(jnp.mean(o * o, axis=0, keepdims=True) + EPS) * gsub_ref[...]
        o_ref[:, hd * EB:(hd + 1) * EB] = (o * (1.0 - lam_init)).T.astype(o_ref.dtype)

    def tile(t):
        items = [(j, hd) for j in range(t + 1) for hd in range(HB)]
        pending = []
        for n in range(B_LOOKAHEAD):
            prep_q(n)
            pending.append(scores(*items[n]))
        for n in range(B_LOOKAHEAD, HB):
            prep_q(n)
        for n, (j, hd) in enumerate(items):
            if n + B_LOOKAHEAD < len(items):
                pending.append(scores(*items[n + B_LOOKAHEAD]))
            update(j, hd, pending.pop(0), j == 0, j == t)
            if j == t:
                finish(hd)

    for t in range(tiles):
        pl.when(i == t)(functools.partial(tile, t))


def _diff_prompt(qt, k, vt, lams, gsub_col, lam_init, bsz):
    w, n = qt.shape
    s = n // bsz
    tiles = s // Q_TILE
    qspec = pl.BlockSpec((w, Q_TILE), lambda b, i: (0, b * tiles + i))
    ospec = pl.BlockSpec((Q_TILE, w), lambda b, i: (b * tiles + i, 0))
    kfull = pl.BlockSpec((s, w), lambda b, i: (b, 0))
    vfull = pl.BlockSpec((w, s), lambda b, i: (0, b))
    vec = _resident((1, DB))
    return pl.pallas_call(
        functools.partial(_diff_prompt_kernel, lam_init=lam_init, tiles=tiles),
        grid=(bsz, tiles),
        in_specs=[qspec, kfull, vfull, vec, vec, vec, vec, _resident((EB, 1))],
        out_specs=ospec,
        out_shape=jax.ShapeDtypeStruct((n, w), BF16),
        scratch_shapes=[pltpu.VMEM((HB, 1, 2 * Q_TILE), F32), pltpu.VMEM((HB, 1, 2 * Q_TILE), F32),
                        pltpu.VMEM((HB, EB, 2 * Q_TILE), F32),
                        pltpu.VMEM((HB, EB, 2 * Q_TILE), BF16),
                        pltpu.VMEM((Q_TILE, 2 * Q_TILE), F32)],
        compiler_params=_cparams("arbitrary", "arbitrary"),
        name="diff_prompt",
    )(qt, k, vt, *lams, gsub_col)


def _diff_sample_kernel(q_ref, kc_ref, vc_ref, kn_ref, vn_ref, lq1_ref, lk1_ref, lq2_ref,
                        lk2_ref, gsub_ref, o_ref, *, lam_init):
    lam = _lam(lq1_ref, lk1_ref, lq2_ref, lk2_ref, lam_init)
    lo, hi, _ = _pair_masks()
    past = kc_ref.shape[0] // HB
    t = q_ref.shape[1]

    def scores(hd):
        ls = slice(hd * EB, (hd + 1) * EB)
        q = q_ref[0, :, ls]
        q12 = jnp.concatenate([q * lo, q * hi], axis=0)
        kc = kc_ref[pl.ds(hd, past, stride=HB), :].astype(BF16)
        return _dot_nt(q12, kc), _dot_nt(q12, kn_ref[0, :, ls])

    def finish(hd, sc, sn):
        ls = slice(hd * EB, (hd + 1) * EB)
        vc = vc_ref[pl.ds(hd, past, stride=HB), :].astype(BF16)
        o12 = _softmax_pv([sc, sn], [lambda p: _dot(p, vc), lambda p: _dot(p, vn_ref[0, :, ls])])
        o = o12[:t] - lam * o12[t:]
        o_ref[0, :, ls] = (_rms(o, gsub_ref[...]) * (1.0 - lam_init)).astype(o_ref.dtype)

    pending = [scores(hd) for hd in range(B_LOOKAHEAD)]
    for hd in range(HB):
        if hd + B_LOOKAHEAD < HB:
            pending.append(scores(hd + B_LOOKAHEAD))
        finish(hd, *pending.pop(0))


def _diff_sample(q, kn, vn, cache_k, cache_v, layer, lams, gsub, lam_init):
    b, t, w = q.shape
    rows = cache_k.shape[2]
    new = pl.BlockSpec((1, t, w), lambda i: (i, 0, 0))
    old = pl.BlockSpec((None, None, rows, EB), lambda i: (layer, i, 0, 0))
    vec = _resident((1, DB))
    return pl.pallas_call(
        functools.partial(_diff_sample_kernel, lam_init=lam_init),
        grid=(b,),
        in_specs=[new, old, old, new, new, vec, vec, vec, vec, _resident((1, EB))],
        out_specs=new,
        out_shape=jax.ShapeDtypeStruct((b, t, w), BF16),
        compiler_params=_cparams("parallel"),
        name="diff_sample",
    )(q, cache_k, cache_v, kn, vn, *lams, gsub)


def _rope_tables(pos):
    half = DB // 2
    inv = ROPE_THETA ** (-jnp.arange(half, dtype=F32) * 2.0 / DB)
    ang = pos.astype(F32)[:, None] * inv[None, :]
    cos, sin = jnp.cos(ang), jnp.sin(ang)
    cos2 = jnp.concatenate([cos, cos], axis=-1)
    sin2 = jnp.concatenate([-sin, sin], axis=-1)
    return jnp.tile(cos2, (1, LANES // DB)), jnp.tile(sin2, (1, LANES // DB))


def _rel_rows(table):
    table = table.astype(F32)
    n_const = REL_SPAN - (2 * MAX_REL - 1)
    const = jnp.broadcast_to(table[:, 2 * MAX_REL:], (HA, n_const))
    desc = jnp.concatenate([table[:, 1:2 * MAX_REL], const], axis=1)
    asc = jnp.concatenate([const, table[:, 1:2 * MAX_REL][:, ::-1]], axis=1)
    return asc, desc


def kernel(x_prompt, x_sample, cache_a_k, cache_a_v, cache_b_k, cache_b_v,
           g_ffn1, w1_gate, w1_up, w1_down, g_mix, w_in, g_qa, g_ka, g_qb, g_kb,
           rel_bias, lam_q1, lam_k1, lam_q2, lam_k2, g_sub, w_out,
           g_ffn2, w2_gate, w2_up, w2_down):
    bp, s, d = x_prompt.shape
    bd, t, _ = x_sample.shape
    depth = w_in.shape[0]
    past = cache_b_k.shape[2]
    a_len = cache_a_k.shape[2]
    assert a_len == BAND_PREV and t == CHUNK and s % Q_TILE == 0 and MAX_REL == LANES
    keep_p = min(BAND_PREV, s)

    cos_p, sin_p = _rope_tables(jnp.arange(s))
    reps = TOKEN_TILE // t
    cos_s, sin_s = (jnp.tile(a, (reps, 1)) for a in _rope_tables(past + jnp.arange(t)))

    cakt = jnp.transpose(cache_a_k, (0, 1, 3, 4, 2))
    cavt = jnp.transpose(cache_a_v, (0, 1, 3, 4, 2))
    cbk = cache_b_k.reshape(depth, bd, past * HB, EB)
    cbv = cache_b_v.reshape(depth, bd, past * HB, EB)

    yp = x_prompt.reshape(bp * s, d)
    ys = x_sample.reshape(bd * t, d)
    pak, pav, sak, sav = [], [], [], []
    pb = sb = None
    for l in range(depth):
        lam_init = 0.8 - 0.6 * math.exp(-0.3 * l)
        row = lambda a: a[l].astype(F32).reshape(1, -1)
        tiled = lambda a: jnp.tile(row(a), (1, GROUP_W // a.shape[1]))
        w1 = (w1_gate[l].astype(BF16), w1_up[l].astype(BF16), w1_down[l].astype(BF16))
        w2 = (w2_gate[l].astype(BF16), w2_up[l].astype(BF16), w2_down[l].astype(BF16))
        wi = w_in[l].astype(BF16)
        woa = w_out[l, :GROUP_W].astype(BF16)
        wob = w_out[l, GROUP_W:].astype(BF16)
        gains = (tiled(g_qa), tiled(g_ka), tiled(g_qb), tiled(g_kb))
        lams = (row(lam_q1), row(lam_k1), row(lam_q2), row(lam_k2))
        gsub = row(g_sub)
        rel_asc, rel_desc = _rel_rows(rel_bias[l])

        yp = _ffn(yp, row(g_ffn1), *w1)
        qat, ka, vat, qbt, kb, vbt, kat32, vat32, *pb = _proj(
            yp, row(g_mix), wi, *gains, cos_p, sin_p, s, l, depth, True, pb)
        oa = _band_prompt(qat, ka, vat, rel_desc, bp)
        ob = _diff_prompt(qbt, kb, vbt, lams, gsub.reshape(EB, 1), lam_init, bp)
        yp = _out_ffn(yp, oa, ob, woa, wob, row(g_ffn2), *w2)
        pak.append(kat32)
        pav.append(vat32)

        ys = _ffn(ys, row(g_ffn1), *w1)
        qa, ka, va, qb, kb, vb, ka32, va32, *sb = _proj(
            ys, row(g_mix), wi, *gains, cos_s, sin_s, t, l, depth, False, sb)
        sh = lambda a: a.reshape(bd, t, GROUP_W)
        oa = _band_sample(sh(qa), sh(ka), sh(va), cakt, cavt, l, rel_asc, 4)
        ob = _diff_sample(sh(qb), sh(kb), sh(vb), cbk, cbv, l, lams, gsub, lam_init)
        ys = _out_ffn(ys, oa.reshape(bd * t, GROUP_W), ob.reshape(bd * t, GROUP_W),
                      woa, wob, row(g_ffn2), *w2)
        sak.append(ka32)
        sav.append(va32)

    def kept_prompt(parts):
        a = jnp.stack(parts).reshape(depth, bp, HA, DA, keep_p)
        return jnp.transpose(a, (0, 1, 4, 2, 3))

    return (yp.reshape(bp, s, d), ys.reshape(bd, t, d),
            kept_prompt(pak), kept_prompt(pav),
            pb[0].reshape(depth, bp, s, HB, EB), pb[1].reshape(depth, bp, s, HB, EB),
            jnp.stack(sak).reshape(depth, bd, t, HA, DA),
            jnp.stack(sav).reshape(depth, bd, t, HA, DA),
            sb[0].reshape(depth, bd, t, HB, EB), sb[1].reshape(depth, bd, t, HB, EB))
```

```python
import functools
import math

import jax
import jax.numpy as jnp
from jax import lax
from jax.experimental import pallas as pl
from jax.experimental.pallas import tpu as pltpu

CHUNK = 64
PREV_CHUNKS = 8
BAND_PREV = PREV_CHUNKS * CHUNK
MAX_REL = 128
HA = 8
DA = 64
HB = 4
DB = 64
EB = 2 * DB
GROUP_W = HA * DA
ROPE_THETA = 10000.0
EPS = 1e-6
NEG = -1e30
LOG2E = math.log2(math.e)

LANES = 128
MXU_DIM = 256
TOKEN_TILE = 512
Q_TILE = 256
A_WIN = BAND_PREV + LANES
A_GROUPS = A_WIN // LANES
REL_SPAN = A_WIN + LANES
A_LOOKAHEAD = 3
B_LOOKAHEAD = 2
VMEM_LIMIT = 56 * 1024 * 1024

F32 = jnp.float32
BF16 = jnp.bfloat16


def _cparams(*sem):
    return pltpu.CompilerParams(dimension_semantics=sem, vmem_limit_bytes=VMEM_LIMIT)


def _resident(shape):
    nd = len(shape)
    return pl.BlockSpec(shape, lambda *_: (0,) * nd, pipeline_mode=pl.Buffered(1))


def _layer_resident(shape, layer, row_block=0):
    return pl.BlockSpec((None,) + tuple(shape), lambda *_: (layer, row_block, 0),
                        pipeline_mode=pl.Buffered(1))


def _dot(a, b):
    return jnp.dot(a, b, preferred_element_type=F32)


def _dot_nt(a, b):
    return lax.dot_general(a, b, (((1,), (1,)), ((), ())), preferred_element_type=F32)


def _rms(x, g):
    return x * lax.rsqrt(jnp.mean(x * x, axis=-1, keepdims=True) + EPS) * g


def _swiglu_half(x, g, wg_ref, wu_ref, wd_ref):
    h = _rms(x, g).astype(BF16)
    d_ff = wg_ref.shape[1]
    split = min(d_ff, pl.cdiv(d_ff, 2 * MXU_DIM) * MXU_DIM)
    y = None
    for lo, hi in ((0, split), (split, d_ff)):
        if lo == hi:
            continue
        gate = _dot(h, wg_ref[:, lo:hi])
        up = _dot(h, wu_ref[:, lo:hi])
        a = (gate / (1.0 + jnp.exp(-gate)) * up).astype(BF16)
        part = _dot(a, wd_ref[lo:hi, :])
        y = part if y is None else y + part
    return x + 0.5 * y


def _ffn_kernel(x_ref, g_ref, wg_ref, wu_ref, wd_ref, o_ref):
    o_ref[...] = _swiglu_half(x_ref[...], g_ref[...], wg_ref, wu_ref, wd_ref)


def _out_ffn_kernel(x_ref, oa_ref, ob_ref, woa_ref, wob_ref, g_ref, wg_ref, wu_ref, wd_ref,
                    o_ref):
    x = x_ref[...] + _dot(oa_ref[...], woa_ref[...]) + _dot(ob_ref[...], wob_ref[...])
    o_ref[...] = _swiglu_half(x, g_ref[...], wg_ref, wu_ref, wd_ref)


def _ffn(x, g, wg, wu, wd, layer):
    n, d = x.shape
    d_ff = wg.shape[2]
    assert n % TOKEN_TILE == 0
    row = pl.BlockSpec((TOKEN_TILE, d), lambda i: (i, 0))
    return pl.pallas_call(
        _ffn_kernel,
        grid=(n // TOKEN_TILE,),
        in_specs=[row, _resident((1, d)), _layer_resident((d, d_ff), layer),
                  _layer_resident((d, d_ff), layer), _layer_resident((d_ff, d), layer)],
        out_specs=row,
        out_shape=jax.ShapeDtypeStruct((n, d), F32),
        compiler_params=_cparams("parallel"),
        name="ffn",
    )(x, g, wg, wu, wd)


def _out_ffn(x, oa, ob, wo, g, wg, wu, wd, layer):
    n, d = x.shape
    d_ff = wg.shape[2]
    row = pl.BlockSpec((TOKEN_TILE, d), lambda i: (i, 0))
    grp = pl.BlockSpec((TOKEN_TILE, GROUP_W), lambda i: (i, 0))
    return pl.pallas_call(
        _out_ffn_kernel,
        grid=(n // TOKEN_TILE,),
        in_specs=[row, grp, grp, _layer_resident((GROUP_W, d), layer, 0),
                  _layer_resident((GROUP_W, d), layer, 1), _resident((1, d)),
                  _layer_resident((d, d_ff), layer), _layer_resident((d, d_ff), layer),
                  _layer_resident((d_ff, d), layer)],
        out_specs=row,
        out_shape=jax.ShapeDtypeStruct((n, d), F32),
        compiler_params=_cparams("parallel"),
        name="out_ffn",
    )(x, oa, ob, wo, wo, g, wg, wu, wd)


def _head_norm(z, gain, ones_bd):
    outs = []
    for c in range(0, z.shape[1], MXU_DIM):
        zc = z[:, c:c + MXU_DIM]
        ss = _dot((zc * zc).astype(BF16), ones_bd)
        outs.append(zc * lax.rsqrt(ss * (1.0 / DA) + EPS) * gain[:, c:c + MXU_DIM])
    return jnp.concatenate(outs, axis=1)


def _rope(y, cos2, sin2):
    first_half = (lax.broadcasted_iota(jnp.int32, (1, LANES), 1) % DB) < (DB // 2)
    outs = []
    for c in range(0, y.shape[1], LANES):
        yc = y[:, c:c + LANES]
        swapped = jnp.where(first_half, pltpu.roll(yc, LANES - DB // 2, 1),
                            pltpu.roll(yc, DB // 2, 1))
        outs.append(yc * cos2 + swapped * sin2)
    return jnp.concatenate(outs, axis=1)


def _store_heads_interleaved(o_ref, slot, val):
    m = val.shape[0]
    for s in range(o_ref.shape[0]):
        if s == slot:
            for hd in range(HB):
                o_ref[s, pl.ds(hd, m, stride=HB), :] = val[:, hd * EB:(hd + 1) * EB]
        else:
            o_ref[s] = jnp.zeros(o_ref.shape[1:], o_ref.dtype)


def _proj_kernel(*refs, tiles_per_seq, transposed, aliased, slot):
    (x_ref, g_ref, w_ref, gqa_ref, gka_ref, gqb_ref, gkb_ref, cos_ref, sin_ref) = refs[:9]
    outs = refs[9 + (4 if aliased else 0):]
    qa_o, ka_o, va_o, qb_o, kb_o, vb_o, ka32_o, va32_o, kb32_o, vb32_o = outs
    h = _rms(x_ref[...], g_ref[...]).astype(BF16)
    r = lax.broadcasted_iota(jnp.int32, (MXU_DIM, MXU_DIM), 0) // DA
    c = lax.broadcasted_iota(jnp.int32, (MXU_DIM, MXU_DIM), 1) // DA
    ones_bd = (r == c).astype(BF16)
    cos2 = cos_ref[...]
    sin2 = sin_ref[...]

    def group(j):
        return _dot(h, w_ref[:, j * GROUP_W:(j + 1) * GROUP_W])

    def maybe_t(a):
        return a.T if transposed else a

    gqa = gqa_ref[...] * (DA ** -0.5 * LOG2E)
    gqb = gqb_ref[...] * (DB ** -0.5 * LOG2E)
    kept = []

    def finish_qa(z):
        qa_o[...] = maybe_t(_head_norm(z, gqa, ones_bd)).astype(BF16)

    def finish_ka(z):
        ka = _head_norm(z, gka_ref[...], ones_bd)
        ka_o[...] = ka.astype(BF16)
        kept.append((ka32_o, maybe_t(ka)))

    def finish_va(z):
        va_t = maybe_t(z)
        va_o[...] = va_t.astype(BF16)
        kept.append((va32_o, va_t))

    def finish_qb(z):
        qb_o[...] = maybe_t(_rope(_head_norm(z, gqb, ones_bd), cos2, sin2)).astype(BF16)

    def finish_kb(z):
        kb = _rope(_head_norm(z, gkb_ref[...], ones_bd), cos2, sin2)
        kb_o[...] = kb.astype(BF16)
        _store_heads_interleaved(kb32_o, slot, kb)

    def finish_vb(z):
        vb_o[...] = maybe_t(z).astype(BF16)
        _store_heads_interleaved(vb32_o, slot, z)

    order = ((3, finish_qb), (4, finish_kb), (0, finish_qa), (1, finish_ka), (5, finish_vb),
             (2, finish_va))
    z = group(order[0][0])
    for n, (_, finish) in enumerate(order):
        nxt = group(order[n + 1][0]) if n + 1 < len(order) else None
        finish(z)
        z = nxt

    @pl.when(pl.program_id(0) % tiles_per_seq == tiles_per_seq - 1)
    def _():
        for o_ref, val in kept:
            for s in range(o_ref.shape[0]):
                o_ref[s] = val if s == slot else jnp.zeros_like(val)


def _proj(x, g, w_in, gqa, gka, gqb, gkb, cos2, sin2, seq, layer, depth, transposed, prev):
    n, d = x.shape
    keep = min(BAND_PREV, seq)
    if seq >= TOKEN_TILE:
        assert seq % TOKEN_TILE == 0 and keep == TOKEN_TILE
        tiles_per_seq = seq // TOKEN_TILE
    else:
        assert TOKEN_TILE % seq == 0 and keep == seq and not transposed
        tiles_per_seq = 1
    n_seq = n // seq
    aliased = prev is not None
    slots, first_slot, slot = (1, layer, 0) if aliased else (depth, 0, layer)
    pos_tiles = cos2.shape[0] // TOKEN_TILE
    row = pl.BlockSpec((TOKEN_TILE, d), lambda i: (i, 0))
    grp = pl.BlockSpec((TOKEN_TILE, GROUP_W), lambda i: (i, 0))
    grp_t = pl.BlockSpec((GROUP_W, TOKEN_TILE), lambda i: (0, i))
    tab = pl.BlockSpec((TOKEN_TILE, LANES), lambda i: (i % pos_tiles, 0))
    gain = _resident((1, GROUP_W))
    nat = jax.ShapeDtypeStruct((n, GROUP_W), BF16)
    tr = jax.ShapeDtypeStruct((GROUP_W, n), BF16)
    if transposed:
        qv_spec, qv_shape = grp_t, tr
        kept = pl.BlockSpec((slots, None, GROUP_W, keep),
                            lambda i: (first_slot, i // tiles_per_seq, 0, 0))
        kept_shape = jax.ShapeDtypeStruct((depth, n_seq, GROUP_W, keep), F32)
    else:
        qv_spec, qv_shape = grp, nat
        kept = pl.BlockSpec((slots, TOKEN_TILE, GROUP_W),
                            lambda i: (first_slot, i // tiles_per_seq, 0))
        kept_shape = jax.ShapeDtypeStruct((depth, n_seq * keep, GROUP_W), F32)
    cache_b = pl.BlockSpec((slots, TOKEN_TILE * HB, EB), lambda i: (first_slot, i, 0))
    cache_b_shape = jax.ShapeDtypeStruct((depth, n * HB, EB), F32)
    any_spec = pl.BlockSpec(memory_space=pl.ANY)
    return pl.pallas_call(
        functools.partial(_proj_kernel, tiles_per_seq=tiles_per_seq, transposed=transposed,
                          aliased=aliased, slot=slot),
        grid=(n // TOKEN_TILE,),
        in_specs=[row, _resident((1, d)), _layer_resident(w_in.shape[1:], layer), gain, gain, gain,
                  gain, tab, tab]
                 + ([any_spec] * 4 if aliased else []),
        out_specs=[qv_spec, grp, qv_spec, qv_spec, grp, qv_spec, kept, kept, cache_b, cache_b],
        out_shape=[qv_shape, nat, qv_shape, qv_shape, nat, qv_shape, kept_shape, kept_shape,
                   cache_b_shape, cache_b_shape],
        input_output_aliases={9: 6, 10: 7, 11: 8, 12: 9} if aliased else {},
        compiler_params=_cparams("arbitrary"),
        name="proj",
    )(x, g, w_in, gqa, gka, gqb, gkb, cos2, sin2, *(prev if aliased else ()))


def _row_half_masks(cols=LANES):
    row = lax.broadcasted_iota(jnp.int32, (LANES, cols), 0)
    top = jnp.where(row < LANES // 2, 1.0, 0.0).astype(BF16)
    bottom = jnp.where(row >= LANES // 2, 1.0, 0.0).astype(BF16)
    return top, bottom


def _toeplitz(tab_c, tab_prev, rows, base_shift):
    lane = lax.broadcasted_iota(jnp.int32, (rows, LANES), 1)
    shift = lax.broadcasted_iota(jnp.int32, (rows, LANES), 0) + base_shift
    cur = pltpu.roll(jnp.broadcast_to(tab_c, (rows, LANES)), base_shift, 1, stride=1, stride_axis=0)
    prev = pltpu.roll(jnp.broadcast_to(tab_prev, (rows, LANES)), base_shift, 1, stride=1,
                      stride_axis=0)
    return jnp.where(lane >= shift, cur, prev)


def _band_prompt_kernel(qt_ref, k0_ref, k1_ref, k2_ref, v0_ref, v1_ref, v2_ref, rel_ref, o_ref,
                        bias_ref):
    i = pl.program_id(1)

    @pl.when((pl.program_id(0) == 0) & (i == 0))
    def _():
        kc = lax.broadcasted_iota(jnp.int32, (LANES, LANES), 0) // CHUNK
        qc = lax.broadcasted_iota(jnp.int32, (LANES, LANES), 1) // CHUNK
        for g in range(A_GROUPS):
            dc = kc + (LANES // CHUNK) * g - qc
            visible = (dc >= 0) & (dc <= PREV_CHUNKS)
            hi = (A_GROUPS - 1 - g) * LANES
            for hd in range(HA):
                t = _toeplitz(rel_ref[hd:hd + 1, hi + LANES:hi + 2 * LANES],
                              rel_ref[hd:hd + 1, hi:hi + LANES], LANES, 1)
                bias_ref[hd, g * LANES:(g + 1) * LANES, :] = jnp.where(visible, t * LOG2E, NEG)
        bias_ref[HA] = jnp.full((A_WIN, LANES), NEG, F32)

    top, bottom = _row_half_masks()
    k_refs = (k0_ref, k1_ref, k2_ref)
    v_refs = (v0_ref, v1_ref, v2_ref)
    halves = Q_TILE // LANES
    items = [(pair, half) for pair in range(HA // 2) for half in range(halves)]

    def scores(pair, half):
        ls = slice(pair * LANES, (pair + 1) * LANES)
        kwin = jnp.concatenate([r[:, ls] for r in k_refs], axis=0)[half * LANES:half * LANES + A_WIN]
        qt = qt_ref[ls, half * LANES:(half + 1) * LANES]
        return _dot(kwin, jnp.concatenate([qt * top, qt * bottom], axis=1))

    def finish(pair, half, st2):
        ls = slice(pair * LANES, (pair + 1) * LANES)
        ps, sums = [], []
        for sub in range(2):
            hd = 2 * pair + sub
            parts = []
            for g in range(A_GROUPS):
                idx = jnp.where(2 * i + half + g >= A_GROUPS - 1, hd, HA)
                parts.append(st2[g * LANES:(g + 1) * LANES, sub * LANES:(sub + 1) * LANES]
                             + bias_ref[idx, g * LANES:(g + 1) * LANES, :])
            s = jnp.concatenate(parts, axis=0)
            p = jnp.exp2(s - jnp.max(s, axis=0, keepdims=True))
            sums.append(jnp.sum(p, axis=0, keepdims=True))
            ps.append(p.astype(BF16))
        vwin = jnp.concatenate([r[ls, :] for r in v_refs], axis=1)[:, half * LANES:half * LANES + A_WIN]
        ot2 = _dot(vwin, jnp.concatenate(ps, axis=1))
        ot = jnp.concatenate([ot2[:DA, :LANES] * (1.0 / sums[0]),
                              ot2[DA:, LANES:] * (1.0 / sums[1])], axis=0)
        o_ref[half * LANES:(half + 1) * LANES, ls] = ot.T.astype(o_ref.dtype)

    pending = [scores(*item) for item in items[:A_LOOKAHEAD]]
    for n, item in enumerate(items):
        if n + A_LOOKAHEAD < len(items):
            pending.append(scores(*items[n + A_LOOKAHEAD]))
        finish(*item, pending.pop(0))


def _band_prompt(qt, k, vt, rel, bsz):
    w, n = qt.shape
    tiles = n // bsz // Q_TILE
    qspec = pl.BlockSpec((w, Q_TILE), lambda b, i: (0, b * tiles + i))
    ospec = pl.BlockSpec((Q_TILE, w), lambda b, i: (b * tiles + i, 0))

    def kspec(back):
        return pl.BlockSpec((Q_TILE, w), lambda b, i: (b * tiles + jnp.maximum(i - back, 0), 0))

    def vspec(back):
        return pl.BlockSpec((w, Q_TILE), lambda b, i: (0, b * tiles + jnp.maximum(i - back, 0)))

    return pl.pallas_call(
        _band_prompt_kernel,
        grid=(bsz, tiles),
        in_specs=[qspec, kspec(2), kspec(1), kspec(0), vspec(2), vspec(1), vspec(0),
                  _resident(rel.shape)],
        out_specs=ospec,
        out_shape=jax.ShapeDtypeStruct((n, w), BF16),
        scratch_shapes=[pltpu.VMEM((HA + 1, A_WIN, LANES), F32)],
        compiler_params=_cparams("arbitrary", "arbitrary"),
        name="band_prompt",
    )(qt, k, k, k, vt, vt, vt, rel)


def _pair_masks():
    lane = lax.broadcasted_iota(jnp.int32, (1, LANES), 1)
    lo = jnp.where(lane < DA, 1.0, 0.0).astype(BF16)
    hi = jnp.where(lane >= DA, 1.0, 0.0).astype(BF16)
    return lo, hi, lane < DA


def _softmax_pv(s_list, pv_list):
    m = None
    for s in s_list:
        mx = jnp.max(s, axis=-1, keepdims=True)
        m = mx if m is None else jnp.maximum(m, mx)
    l = None
    o = None
    for s, pv in zip(s_list, pv_list):
        p = jnp.exp2(s - m)
        ps = jnp.sum(p, axis=-1, keepdims=True)
        po = pv(p.astype(BF16))
        l = ps if l is None else l + ps
        o = po if o is None else o + po
    return o * (1.0 / l)


def _band_sample_kernel(q_ref, kct_ref, vct_ref, kn_ref, vn_ref, rel_ref, o_ref, bias_ref):
    t = q_ref.shape[1]

    @pl.when(pl.program_id(0) == 0)
    def _():
        for g in range(A_GROUPS):
            lo = (g + 1) * LANES
            for hd in range(HA):
                bias_ref[hd // 2, (hd % 2) * t:(hd % 2 + 1) * t, g * LANES:(g + 1) * LANES] = (
                    LOG2E * _toeplitz(rel_ref[hd:hd + 1, lo:lo + LANES],
                                      rel_ref[hd:hd + 1, lo - LANES:lo], t, 0))

    lo, hi, is_lo = _pair_masks()
    items = [(bi, pair) for bi in range(q_ref.shape[0]) for pair in range(HA // 2)]

    def scores(bi, pair):
        ls = slice(pair * LANES, (pair + 1) * LANES)
        q = q_ref[bi, :, ls]
        q2 = jnp.concatenate([q * lo, q * hi], axis=0)
        kct = kct_ref[bi, 2 * pair:2 * pair + 2].reshape(LANES, BAND_PREV).astype(BF16)
        return (_dot(q2, kct) + bias_ref[pair, :, :BAND_PREV],
                _dot_nt(q2, kn_ref[bi, :, ls]) + bias_ref[pair, :, BAND_PREV:BAND_PREV + t])

    def finish(bi, pair, sc, sn):
        ls = slice(pair * LANES, (pair + 1) * LANES)
        vct = vct_ref[bi, 2 * pair:2 * pair + 2].reshape(LANES, BAND_PREV).astype(BF16)
        o2 = _softmax_pv([sc, sn], [lambda p: _dot_nt(p, vct), lambda p: _dot(p, vn_ref[bi, :, ls])])
        o_ref[bi, :, ls] = jnp.where(is_lo, o2[:t], o2[t:]).astype(o_ref.dtype)

    pending = [scores(*item) for item in items[:A_LOOKAHEAD]]
    for n, item in enumerate(items):
        if n + A_LOOKAHEAD < len(items):
            pending.append(scores(*items[n + A_LOOKAHEAD]))
        finish(*item, *pending.pop(0))


def _band_sample(q, kn, vn, cache_kt, cache_vt, layer, rel, group):
    b, t, w = q.shape
    a_len = cache_kt.shape[4]
    assert b % group == 0 and a_len == BAND_PREV and t == CHUNK
    new = pl.BlockSpec((group, t, w), lambda i: (i, 0, 0))
    old = pl.BlockSpec((None, group, HA, DA, a_len), lambda i: (layer, i, 0, 0, 0))
    return pl.pallas_call(
        _band_sample_kernel,
        grid=(b // group,),
        in_specs=[new, old, old, new, new, _resident(rel.shape)],
        out_specs=new,
        out_shape=jax.ShapeDtypeStruct((b, t, w), BF16),
        scratch_shapes=[pltpu.VMEM((HA // 2, 2 * t, A_WIN), F32)],
        compiler_params=_cparams("arbitrary"),
        name="band_sample",
    )(q, cache_kt, cache_vt, kn, vn, rel)


def _lam(lq1_ref, lk1_ref, lq2_ref, lk2_ref, lam_init):
    s1 = jnp.sum(lq1_ref[...] * lk1_ref[...], axis=-1, keepdims=True)
    s2 = jnp.sum(lq2_ref[...] * lk2_ref[...], axis=-1, keepdims=True)
    return jnp.exp(s1) - jnp.exp(s2) + lam_init


def _diff_prompt_kernel(qt_ref, k_ref, vt_ref, lq1_ref, lk1_ref, lq2_ref, lk2_ref, gsub_ref,
                        o_ref, m_ref, l_ref, acc_ref, q2_ref, diag_ref, *, lam_init, tiles):
    i = pl.program_id(1)
    lam = _lam(lq1_ref, lk1_ref, lq2_ref, lk2_ref, lam_init)
    top, bottom = _row_half_masks(Q_TILE)

    @pl.when((pl.program_id(0) == 0) & (i == 0))
    def _():
        kc = lax.broadcasted_iota(jnp.int32, (Q_TILE, 2 * Q_TILE), 0) // CHUNK
        qc = (lax.broadcasted_iota(jnp.int32, (Q_TILE, 2 * Q_TILE), 1) % Q_TILE) // CHUNK
        diag_ref[...] = jnp.where(kc <= qc, 0.0, NEG).astype(F32)

    def prep_q(hd):
        qt = qt_ref[hd * EB:(hd + 1) * EB, :]
        q2_ref[hd] = jnp.concatenate([qt * top, qt * bottom], axis=1)

    def scores(j, hd):
        return _dot(k_ref[j * Q_TILE:(j + 1) * Q_TILE, hd * EB:(hd + 1) * EB], q2_ref[hd])

    def update(j, hd, st, first, diagonal):
        if diagonal:
            st = st + diag_ref[...]
        vt = vt_ref[hd * EB:(hd + 1) * EB, j * Q_TILE:(j + 1) * Q_TILE]
        m_new = jnp.max(st, axis=0, keepdims=True)
        if first:
            p = jnp.exp2(st - m_new)
            l_ref[hd] = jnp.sum(p, axis=0, keepdims=True)
            acc_ref[hd] = _dot(vt, p.astype(BF16))
        else:
            m_old = m_ref[hd]
            m_new = jnp.maximum(m_old, m_new)
            p = jnp.exp2(st - m_new)
            alpha = jnp.exp2(m_old - m_new)
            l_ref[hd] = alpha * l_ref[hd] + jnp.sum(p, axis=0, keepdims=True)
            acc_ref[hd] = alpha * acc_ref[hd] + _dot(vt, p.astype(BF16))
        m_ref[hd] = m_new

    def finish(hd):
        o12 = acc_ref[hd] * (1.0 / l_ref[hd])
        o = o12[:, :Q_TILE] - lam * o12[:, Q_TILE:]
        o = o * lax.rsqrt(jnp.mean(o * o, axis=0, keepdims=True) + EPS) * gsub_ref[...]
        o_ref[:, hd * EB:(hd + 1) * EB] = (o * (1.0 - lam_init)).T.astype(o_ref.dtype)

    def tile(t):
        items = [(j, hd) for j in range(t + 1) for hd in range(HB)]
        pending = []
        for n in range(B_LOOKAHEAD):
            prep_q(n)
            pending.append(scores(*items[n]))
        for n in range(B_LOOKAHEAD, HB):
            prep_q(n)
        for n, (j, hd) in enumerate(items):
            if n + B_LOOKAHEAD < len(items):
                pending.append(scores(*items[n + B_LOOKAHEAD]))
            update(j, hd, pending.pop(0), j == 0, j == t)
            if j == t:
                finish(hd)

    for t in range(tiles):
        pl.when(i == t)(functools.partial(tile, t))


def _diff_prompt(qt, k, vt, lams, gsub_col, lam_init, bsz):
    w, n = qt.shape
    s = n // bsz
    tiles = s // Q_TILE
    qspec = pl.BlockSpec((w, Q_TILE), lambda b, i: (0, b * tiles + i))
    ospec = pl.BlockSpec((Q_TILE, w), lambda b, i: (b * tiles + i, 0))
    kfull = pl.BlockSpec((s, w), lambda b, i: (b, 0))
    vfull = pl.BlockSpec((w, s), lambda b, i: (0, b))
    vec = _resident((1, DB))
    return pl.pallas_call(
        functools.partial(_diff_prompt_kernel, lam_init=lam_init, tiles=tiles),
        grid=(bsz, tiles),
        in_specs=[qspec, kfull, vfull, vec, vec, vec, vec, _resident((EB, 1))],
        out_specs=ospec,
        out_shape=jax.ShapeDtypeStruct((n, w), BF16),
        scratch_shapes=[pltpu.VMEM((HB, 1, 2 * Q_TILE), F32), pltpu.VMEM((HB, 1, 2 * Q_TILE), F32),
                        pltpu.VMEM((HB, EB, 2 * Q_TILE), F32),
                        pltpu.VMEM((HB, EB, 2 * Q_TILE), BF16),
                        pltpu.VMEM((Q_TILE, 2 * Q_TILE), F32)],
        compiler_params=_cparams("arbitrary", "arbitrary"),
        name="diff_prompt",
    )(qt, k, vt, *lams, gsub_col)


def _diff_sample_kernel(q_ref, kc_ref, vc_ref, kn_ref, vn_ref, lq1_ref, lk1_ref, lq2_ref,
                        lk2_ref, gsub_ref, o_ref, *, lam_init):
    lam = _lam(lq1_ref, lk1_ref, lq2_ref, lk2_ref, lam_init)
    lo, hi, _ = _pair_masks()
    past = kc_ref.shape[0] // HB
    t = q_ref.shape[1]

    def scores(hd):
        ls = slice(hd * EB, (hd + 1) * EB)
        q = q_ref[0, :, ls]
        q12 = jnp.concatenate([q * lo, q * hi], axis=0)
        kc = kc_ref[pl.ds(hd, past, stride=HB), :].astype(BF16)
        return _dot_nt(q12, kc), _dot_nt(q12, kn_ref[0, :, ls])

    def finish(hd, sc, sn):
        ls = slice(hd * EB, (hd + 1) * EB)
        vc = vc_ref[pl.ds(hd, past, stride=HB), :].astype(BF16)
        o12 = _softmax_pv([sc, sn], [lambda p: _dot(p, vc), lambda p: _dot(p, vn_ref[0, :, ls])])
        o = o12[:t] - lam * o12[t:]
        o_ref[0, :, ls] = (_rms(o, gsub_ref[...]) * (1.0 - lam_init)).astype(o_ref.dtype)

    pending = [scores(hd) for hd in range(B_LOOKAHEAD)]
    for hd in range(HB):
        if hd + B_LOOKAHEAD < HB:
            pending.append(scores(hd + B_LOOKAHEAD))
        finish(hd, *pending.pop(0))


def _diff_sample(q, kn, vn, cache_k, cache_v, layer, lams, gsub, lam_init):
    b, t, w = q.shape
    rows = cache_k.shape[2]
    new = pl.BlockSpec((1, t, w), lambda i: (i, 0, 0))
    old = pl.BlockSpec((None, None, rows, EB), lambda i: (layer, i, 0, 0))
    vec = _resident((1, DB))
    return pl.pallas_call(
        functools.partial(_diff_sample_kernel, lam_init=lam_init),
        grid=(b,),
        in_specs=[new, old, old, new, new, vec, vec, vec, vec, _resident((1, EB))],
        out_specs=new,
        out_shape=jax.ShapeDtypeStruct((b, t, w), BF16),
        compiler_params=_cparams("parallel"),
        name="diff_sample",
    )(q, cache_k, cache_v, kn, vn, *lams, gsub)


def _rope_tables(pos):
    half = DB // 2
    inv = ROPE_THETA ** (-jnp.arange(half, dtype=F32) * 2.0 / DB)
    ang = pos.astype(F32)[:, None] * inv[None, :]
    cos, sin = jnp.cos(ang), jnp.sin(ang)
    cos2 = jnp.concatenate([cos, cos], axis=-1)
    sin2 = jnp.concatenate([-sin, sin], axis=-1)
    return jnp.tile(cos2, (1, LANES // DB)), jnp.tile(sin2, (1, LANES // DB))


def _rel_rows(table):
    table = table.astype(F32)
    n_const = REL_SPAN - (2 * MAX_REL - 1)
    const = jnp.broadcast_to(table[:, 2 * MAX_REL:], (HA, n_const))
    desc = jnp.concatenate([table[:, 1:2 * MAX_REL], const], axis=1)
    asc = jnp.concatenate([const, table[:, 1:2 * MAX_REL][:, ::-1]], axis=1)
    return asc, desc


def kernel(x_prompt, x_sample, cache_a_k, cache_a_v, cache_b_k, cache_b_v,
           g_ffn1, w1_gate, w1_up, w1_down, g_mix, w_in, g_qa, g_ka, g_qb, g_kb,
           rel_bias, lam_q1, lam_k1, lam_q2, lam_k2, g_sub, w_out,
           g_ffn2, w2_gate, w2_up, w2_down):
    bp, s, d = x_prompt.shape
    bd, t, _ = x_sample.shape
    depth = w_in.shape[0]
    past = cache_b_k.shape[2]
    a_len = cache_a_k.shape[2]
    assert a_len == BAND_PREV and t == CHUNK and s % Q_TILE == 0 and MAX_REL == LANES
    keep_p = min(BAND_PREV, s)

    cos_p, sin_p = _rope_tables(jnp.arange(s))
    reps = TOKEN_TILE // t
    cos_s, sin_s = (jnp.tile(a, (reps, 1)) for a in _rope_tables(past + jnp.arange(t)))

    cakt = jnp.transpose(cache_a_k, (0, 1, 3, 4, 2))
    cavt = jnp.transpose(cache_a_v, (0, 1, 3, 4, 2))
    cbk = cache_b_k.reshape(depth, bd, past * HB, EB)
    cbv = cache_b_v.reshape(depth, bd, past * HB, EB)

    yp = x_prompt.reshape(bp * s, d)
    ys = x_sample.reshape(bd * t, d)
    pb = sb = None
    w1 = (w1_gate.astype(BF16), w1_up.astype(BF16), w1_down.astype(BF16))
    w2 = (w2_gate.astype(BF16), w2_up.astype(BF16), w2_down.astype(BF16))
    wi = w_in.astype(BF16)
    wo = w_out.astype(BF16)
    for l in range(depth):
        lam_init = 0.8 - 0.6 * math.exp(-0.3 * l)
        row = lambda a: a[l].astype(F32).reshape(1, -1)
        tiled = lambda a: jnp.tile(row(a), (1, GROUP_W // a.shape[1]))
        gains = (tiled(g_qa), tiled(g_ka), tiled(g_qb), tiled(g_kb))
        lams = (row(lam_q1), row(lam_k1), row(lam_q2), row(lam_k2))
        gsub = row(g_sub)
        rel_asc, rel_desc = _rel_rows(rel_bias[l])

        yp = _ffn(yp, row(g_ffn1), *w1, l)
        qat, ka, vat, qbt, kb, vbt, *pb = _proj(
            yp, row(g_mix), wi, *gains, cos_p, sin_p, s, l, depth, True, pb)
        oa = _band_prompt(qat, ka, vat, rel_desc, bp)
        ob = _diff_prompt(qbt, kb, vbt, lams, gsub.reshape(EB, 1), lam_init, bp)
        yp = _out_ffn(yp, oa, ob, wo, row(g_ffn2), *w2, l)

        ys = _ffn(ys, row(g_ffn1), *w1, l)
        qa, ka, va, qb, kb, vb, *sb = _proj(
            ys, row(g_mix), wi, *gains, cos_s, sin_s, t, l, depth, False, sb)
        sh = lambda a: a.reshape(bd, t, GROUP_W)
        oa = _band_sample(sh(qa), sh(ka), sh(va), cakt, cavt, l, rel_asc, 4)
        ob = _diff_sample(sh(qb), sh(kb), sh(vb), cbk, cbv, l, lams, gsub, lam_init)
        ys = _out_ffn(ys, oa.reshape(bd * t, GROUP_W), ob.reshape(bd * t, GROUP_W),
                      wo, row(g_ffn2), *w2, l)

    def kept_prompt(a):
        return jnp.transpose(a.reshape(depth, bp, HA, DA, keep_p), (0, 1, 4, 2, 3))

    return (yp.reshape(bp, s, d), ys.reshape(bd, t, d),
            kept_prompt(pb[0]), kept_prompt(pb[1]),
            pb[2].reshape(depth, bp, s, HB, EB), pb[3].reshape(depth, bp, s, HB, EB),
            sb[0].reshape(depth, bd, t, HA, DA), sb[1].reshape(depth, bd, t, HA, DA),
            sb[2].reshape(depth, bd, t, HB, EB), sb[3].reshape(depth, bd, t, HB, EB))
```

```python
import functools
import math

import jax
import jax.numpy as jnp
from jax import lax
from jax.experimental import pallas as pl
from jax.experimental.pallas import tpu as pltpu

CHUNK = 64
PREV_CHUNKS = 8
BAND_PREV = PREV_CHUNKS * CHUNK
MAX_REL = 128
HA = 8
DA = 64
HB = 4
DB = 64
EB = 2 * DB
GROUP_W = HA * DA
ROPE_THETA = 10000.0
EPS = 1e-6
NEG = -1e30
LOG2E = math.log2(math.e)

LANES = 128
MXU_DIM = 256
TOKEN_TILE = 512
Q_TILE = 256
A_WIN = BAND_PREV + LANES
A_GROUPS = A_WIN // LANES
REL_SPAN = A_WIN + LANES
A_LOOKAHEAD = 3
B_LOOKAHEAD = 2
VMEM_LIMIT = 56 * 1024 * 1024

F32 = jnp.float32
BF16 = jnp.bfloat16


def _cparams(*sem):
    return pltpu.CompilerParams(dimension_semantics=sem, vmem_limit_bytes=VMEM_LIMIT)


def _resident(shape):
    nd = len(shape)
    return pl.BlockSpec(shape, lambda *_: (0,) * nd, pipeline_mode=pl.Buffered(1))


def _layer_resident(shape, layer, row_block=0):
    return pl.BlockSpec((None,) + tuple(shape), lambda *_: (layer, row_block, 0),
                        pipeline_mode=pl.Buffered(1))


def _dot(a, b):
    return jnp.dot(a, b, preferred_element_type=F32)


def _dot_nt(a, b):
    return lax.dot_general(a, b, (((1,), (1,)), ((), ())), preferred_element_type=F32)


def _rms(x, g):
    return x * lax.rsqrt(jnp.mean(x * x, axis=-1, keepdims=True) + EPS) * g


def _swiglu_half(x, g, wg_ref, wu_ref, wd_ref):
    h = _rms(x, g).astype(BF16)
    d_ff = wg_ref.shape[1]
    split = min(d_ff, pl.cdiv(d_ff, 2 * MXU_DIM) * MXU_DIM)
    y = None
    for lo, hi in ((0, split), (split, d_ff)):
        if lo == hi:
            continue
        gate = _dot(h, wg_ref[:, lo:hi])
        up = _dot(h, wu_ref[:, lo:hi])
        a = (gate / (1.0 + jnp.exp(-gate)) * up).astype(BF16)
        part = _dot(a, wd_ref[lo:hi, :])
        y = part if y is None else y + part
    return x + 0.5 * y


def _ffn_kernel(x_ref, g_ref, wg_ref, wu_ref, wd_ref, o_ref):
    o_ref[...] = _swiglu_half(x_ref[...], g_ref[...], wg_ref, wu_ref, wd_ref)


def _out_ffn_kernel(x_ref, oa_ref, ob_ref, woa_ref, wob_ref, g_ref, wg_ref, wu_ref, wd_ref,
                    o_ref):
    x = x_ref[...] + _dot(oa_ref[...], woa_ref[...]) + _dot(ob_ref[...], wob_ref[...])
    o_ref[...] = _swiglu_half(x, g_ref[...], wg_ref, wu_ref, wd_ref)


def _ffn(x, g, wg, wu, wd, layer):
    n, d = x.shape
    d_ff = wg.shape[2]
    assert n % TOKEN_TILE == 0
    row = pl.BlockSpec((TOKEN_TILE, d), lambda i: (i, 0))
    return pl.pallas_call(
        _ffn_kernel,
        grid=(n // TOKEN_TILE,),
        in_specs=[row, _resident((1, d)), _layer_resident((d, d_ff), layer),
                  _layer_resident((d, d_ff), layer), _layer_resident((d_ff, d), layer)],
        out_specs=row,
        out_shape=jax.ShapeDtypeStruct((n, d), F32),
        compiler_params=_cparams("parallel"),
        name="ffn",
    )(x, g, wg, wu, wd)


def _out_ffn(x, oa, ob, wo, g, wg, wu, wd, layer):
    n, d = x.shape
    d_ff = wg.shape[2]
    row = pl.BlockSpec((TOKEN_TILE, d), lambda i: (i, 0))
    grp = pl.BlockSpec((TOKEN_TILE, GROUP_W), lambda i: (i, 0))
    return pl.pallas_call(
        _out_ffn_kernel,
        grid=(n // TOKEN_TILE,),
        in_specs=[row, grp, grp, _layer_resident((GROUP_W, d), layer, 0),
                  _layer_resident((GROUP_W, d), layer, 1), _resident((1, d)),
                  _layer_resident((d, d_ff), layer), _layer_resident((d, d_ff), layer),
                  _layer_resident((d_ff, d), layer)],
        out_specs=row,
        out_shape=jax.ShapeDtypeStruct((n, d), F32),
        compiler_params=_cparams("parallel"),
        name="out_ffn",
    )(x, oa, ob, wo, wo, g, wg, wu, wd)


def _head_norm(z, gain, ones_bd):
    outs = []
    for c in range(0, z.shape[1], MXU_DIM):
        zc = z[:, c:c + MXU_DIM]
        ss = _dot((zc * zc).astype(BF16), ones_bd)
        outs.append(zc * lax.rsqrt(ss * (1.0 / DA) + EPS) * gain[:, c:c + MXU_DIM])
    return jnp.concatenate(outs, axis=1)


def _rope(y, cos2, sin2):
    first_half = (lax.broadcasted_iota(jnp.int32, (1, LANES), 1) % DB) < (DB // 2)
    outs = []
    for c in range(0, y.shape[1], LANES):
        yc = y[:, c:c + LANES]
        swapped = jnp.where(first_half, pltpu.roll(yc, LANES - DB // 2, 1),
                            pltpu.roll(yc, DB // 2, 1))
        outs.append(yc * cos2 + swapped * sin2)
    return jnp.concatenate(outs, axis=1)


def _store_heads_interleaved(o_ref, slot, val):
    m = val.shape[0]
    for s in range(o_ref.shape[0]):
        if s == slot:
            for hd in range(HB):
                o_ref[s, pl.ds(hd, m, stride=HB), :] = val[:, hd * EB:(hd + 1) * EB]
        else:
            o_ref[s] = jnp.zeros(o_ref.shape[1:], o_ref.dtype)


def _proj_kernel(*refs, tiles_per_seq, transposed, aliased, slot):
    (x_ref, g_ref, w_ref, gqa_ref, gka_ref, gqb_ref, gkb_ref, cos_ref, sin_ref) = refs[:9]
    outs = refs[9 + (4 if aliased else 0):]
    qa_o, ka_o, va_o, qb_o, kb_o, vb_o, ka32_o, va32_o, kb32_o, vb32_o = outs
    h = _rms(x_ref[...], g_ref[...]).astype(BF16)
    r = lax.broadcasted_iota(jnp.int32, (MXU_DIM, MXU_DIM), 0) // DA
    c = lax.broadcasted_iota(jnp.int32, (MXU_DIM, MXU_DIM), 1) // DA
    ones_bd = (r == c).astype(BF16)
    cos2 = cos_ref[...]
    sin2 = sin_ref[...]

    def group(j):
        return _dot(h, w_ref[:, j * GROUP_W:(j + 1) * GROUP_W])

    def maybe_t(a):
        return a.T if transposed else a

    gqa = gqa_ref[...] * (DA ** -0.5 * LOG2E)
    gqb = gqb_ref[...] * (DB ** -0.5 * LOG2E)
    kept = []

    def finish_qa(z):
        qa_o[...] = maybe_t(_head_norm(z, gqa, ones_bd)).astype(BF16)

    def finish_ka(z):
        ka = _head_norm(z, gka_ref[...], ones_bd)
        ka_o[...] = ka.astype(BF16)
        kept.append((ka32_o, maybe_t(ka)))

    def finish_va(z):
        va_t = maybe_t(z)
        va_o[...] = va_t.astype(BF16)
        kept.append((va32_o, va_t))

    def finish_qb(z):
        qb_o[...] = _rope(_head_norm(z, gqb, ones_bd), cos2, sin2).astype(BF16)

    def finish_kb(z):
        kb = _rope(_head_norm(z, gkb_ref[...], ones_bd), cos2, sin2)
        kb_o[...] = maybe_t(kb).astype(BF16)
        _store_heads_interleaved(kb32_o, slot, kb)

    def finish_vb(z):
        vb_o[...] = z.astype(BF16)
        _store_heads_interleaved(vb32_o, slot, z)

    order = ((3, finish_qb), (4, finish_kb), (0, finish_qa), (1, finish_ka), (5, finish_vb),
             (2, finish_va))
    z = group(order[0][0])
    for n, (_, finish) in enumerate(order):
        nxt = group(order[n + 1][0]) if n + 1 < len(order) else None
        finish(z)
        z = nxt

    @pl.when(pl.program_id(0) % tiles_per_seq == tiles_per_seq - 1)
    def _():
        for o_ref, val in kept:
            for s in range(o_ref.shape[0]):
                o_ref[s] = val if s == slot else jnp.zeros_like(val)


def _proj(x, g, w_in, gqa, gka, gqb, gkb, cos2, sin2, seq, layer, depth, transposed, prev):
    n, d = x.shape
    keep = min(BAND_PREV, seq)
    if seq >= TOKEN_TILE:
        assert seq % TOKEN_TILE == 0 and keep == TOKEN_TILE
        tiles_per_seq = seq // TOKEN_TILE
    else:
        assert TOKEN_TILE % seq == 0 and keep == seq and not transposed
        tiles_per_seq = 1
    n_seq = n // seq
    aliased = prev is not None
    slots, first_slot, slot = (1, layer, 0) if aliased else (depth, 0, layer)
    pos_tiles = cos2.shape[0] // TOKEN_TILE
    row = pl.BlockSpec((TOKEN_TILE, d), lambda i: (i, 0))
    grp = pl.BlockSpec((TOKEN_TILE, GROUP_W), lambda i: (i, 0))
    grp_t = pl.BlockSpec((GROUP_W, TOKEN_TILE), lambda i: (0, i))
    tab = pl.BlockSpec((TOKEN_TILE, LANES), lambda i: (i % pos_tiles, 0))
    gain = _resident((1, GROUP_W))
    nat = jax.ShapeDtypeStruct((n, GROUP_W), BF16)
    tr = jax.ShapeDtypeStruct((GROUP_W, n), BF16)
    if transposed:
        qv_spec, qv_shape = grp_t, tr
        kept = pl.BlockSpec((slots, None, GROUP_W, keep),
                            lambda i: (first_slot, i // tiles_per_seq, 0, 0))
        kept_shape = jax.ShapeDtypeStruct((depth, n_seq, GROUP_W, keep), F32)
    else:
        qv_spec, qv_shape = grp, nat
        kept = pl.BlockSpec((slots, TOKEN_TILE, GROUP_W),
                            lambda i: (first_slot, i // tiles_per_seq, 0))
        kept_shape = jax.ShapeDtypeStruct((depth, n_seq * keep, GROUP_W), F32)
    cache_b = pl.BlockSpec((slots, TOKEN_TILE * HB, EB), lambda i: (first_slot, i, 0))
    cache_b_shape = jax.ShapeDtypeStruct((depth, n * HB, EB), F32)
    any_spec = pl.BlockSpec(memory_space=pl.ANY)
    return pl.pallas_call(
        functools.partial(_proj_kernel, tiles_per_seq=tiles_per_seq, transposed=transposed,
                          aliased=aliased, slot=slot),
        grid=(n // TOKEN_TILE,),
        in_specs=[row, _resident((1, d)), _layer_resident(w_in.shape[1:], layer), gain, gain, gain,
                  gain, tab, tab]
                 + ([any_spec] * 4 if aliased else []),
        out_specs=[qv_spec, grp, qv_spec, grp, qv_spec, grp, kept, kept, cache_b, cache_b],
        out_shape=[qv_shape, nat, qv_shape, nat, qv_shape, nat, kept_shape, kept_shape,
                   cache_b_shape, cache_b_shape],
        input_output_aliases={9: 6, 10: 7, 11: 8, 12: 9} if aliased else {},
        compiler_params=_cparams("arbitrary"),
        name="proj",
    )(x, g, w_in, gqa, gka, gqb, gkb, cos2, sin2, *(prev if aliased else ()))


def _row_half_masks(cols=LANES):
    row = lax.broadcasted_iota(jnp.int32, (LANES, cols), 0)
    top = jnp.where(row < LANES // 2, 1.0, 0.0).astype(BF16)
    bottom = jnp.where(row >= LANES // 2, 1.0, 0.0).astype(BF16)
    return top, bottom


def _toeplitz(tab_c, tab_prev, rows, base_shift):
    lane = lax.broadcasted_iota(jnp.int32, (rows, LANES), 1)
    shift = lax.broadcasted_iota(jnp.int32, (rows, LANES), 0) + base_shift
    cur = pltpu.roll(jnp.broadcast_to(tab_c, (rows, LANES)), base_shift, 1, stride=1, stride_axis=0)
    prev = pltpu.roll(jnp.broadcast_to(tab_prev, (rows, LANES)), base_shift, 1, stride=1,
                      stride_axis=0)
    return jnp.where(lane >= shift, cur, prev)


def _band_prompt_kernel(qt_ref, k0_ref, k1_ref, k2_ref, v0_ref, v1_ref, v2_ref, rel_ref, o_ref,
                        bias_ref):
    i = pl.program_id(1)

    @pl.when((pl.program_id(0) == 0) & (i == 0))
    def _():
        kc = lax.broadcasted_iota(jnp.int32, (LANES, LANES), 0) // CHUNK
        qc = lax.broadcasted_iota(jnp.int32, (LANES, LANES), 1) // CHUNK
        for g in range(A_GROUPS):
            dc = kc + (LANES // CHUNK) * g - qc
            visible = (dc >= 0) & (dc <= PREV_CHUNKS)
            hi = (A_GROUPS - 1 - g) * LANES
            for hd in range(HA):
                t = _toeplitz(rel_ref[hd:hd + 1, hi + LANES:hi + 2 * LANES],
                              rel_ref[hd:hd + 1, hi:hi + LANES], LANES, 1)
                bias_ref[hd, g * LANES:(g + 1) * LANES, :] = jnp.where(visible, t * LOG2E, NEG)
        bias_ref[HA] = jnp.full((A_WIN, LANES), NEG, F32)

    top, bottom = _row_half_masks()
    k_refs = (k0_ref, k1_ref, k2_ref)
    v_refs = (v0_ref, v1_ref, v2_ref)
    halves = Q_TILE // LANES
    items = [(pair, half) for pair in range(HA // 2) for half in range(halves)]

    def scores(pair, half):
        ls = slice(pair * LANES, (pair + 1) * LANES)
        kwin = jnp.concatenate([r[:, ls] for r in k_refs], axis=0)[half * LANES:half * LANES + A_WIN]
        qt = qt_ref[ls, half * LANES:(half + 1) * LANES]
        return _dot(kwin, jnp.concatenate([qt * top, qt * bottom], axis=1))

    def finish(pair, half, st2):
        ls = slice(pair * LANES, (pair + 1) * LANES)
        ps, sums = [], []
        for sub in range(2):
            hd = 2 * pair + sub
            parts = []
            for g in range(A_GROUPS):
                idx = jnp.where(2 * i + half + g >= A_GROUPS - 1, hd, HA)
                parts.append(st2[g * LANES:(g + 1) * LANES, sub * LANES:(sub + 1) * LANES]
                             + bias_ref[idx, g * LANES:(g + 1) * LANES, :])
            s = jnp.concatenate(parts, axis=0)
            p = jnp.exp2(s - jnp.max(s, axis=0, keepdims=True))
            sums.append(jnp.sum(p, axis=0, keepdims=True))
            ps.append(p.astype(BF16))
        vwin = jnp.concatenate([r[ls, :] for r in v_refs], axis=1)[:, half * LANES:half * LANES + A_WIN]
        ot2 = _dot(vwin, jnp.concatenate(ps, axis=1))
        ot = jnp.concatenate([ot2[:DA, :LANES] * (1.0 / sums[0]),
                              ot2[DA:, LANES:] * (1.0 / sums[1])], axis=0)
        o_ref[half * LANES:(half + 1) * LANES, ls] = ot.T.astype(o_ref.dtype)

    pending = [scores(*item) for item in items[:A_LOOKAHEAD]]
    for n, item in enumerate(items):
        if n + A_LOOKAHEAD < len(items):
            pending.append(scores(*items[n + A_LOOKAHEAD]))
        finish(*item, pending.pop(0))


def _band_prompt(qt, k, vt, rel, bsz):
    w, n = qt.shape
    tiles = n // bsz // Q_TILE
    qspec = pl.BlockSpec((w, Q_TILE), lambda b, i: (0, b * tiles + i))
    ospec = pl.BlockSpec((Q_TILE, w), lambda b, i: (b * tiles + i, 0))

    def kspec(back):
        return pl.BlockSpec((Q_TILE, w), lambda b, i: (b * tiles + jnp.maximum(i - back, 0), 0))

    def vspec(back):
        return pl.BlockSpec((w, Q_TILE), lambda b, i: (0, b * tiles + jnp.maximum(i - back, 0)))

    return pl.pallas_call(
        _band_prompt_kernel,
        grid=(bsz, tiles),
        in_specs=[qspec, kspec(2), kspec(1), kspec(0), vspec(2), vspec(1), vspec(0),
                  _resident(rel.shape)],
        out_specs=ospec,
        out_shape=jax.ShapeDtypeStruct((n, w), BF16),
        scratch_shapes=[pltpu.VMEM((HA + 1, A_WIN, LANES), F32)],
        compiler_params=_cparams("arbitrary", "arbitrary"),
        name="band_prompt",
    )(qt, k, k, k, vt, vt, vt, rel)


def _pair_masks():
    lane = lax.broadcasted_iota(jnp.int32, (1, LANES), 1)
    lo = jnp.where(lane < DA, 1.0, 0.0).astype(BF16)
    hi = jnp.where(lane >= DA, 1.0, 0.0).astype(BF16)
    return lo, hi, lane < DA


def _softmax_pv(s_list, pv_list):
    m = None
    for s in s_list:
        mx = jnp.max(s, axis=-1, keepdims=True)
        m = mx if m is None else jnp.maximum(m, mx)
    l = None
    o = None
    for s, pv in zip(s_list, pv_list):
        p = jnp.exp2(s - m)
        ps = jnp.sum(p, axis=-1, keepdims=True)
        po = pv(p.astype(BF16))
        l = ps if l is None else l + ps
        o = po if o is None else o + po
    return o * (1.0 / l)


def _band_sample_kernel(q_ref, kct_ref, vct_ref, kn_ref, vn_ref, rel_ref, o_ref, bias_ref):
    t = q_ref.shape[1]

    @pl.when(pl.program_id(0) == 0)
    def _():
        for g in range(A_GROUPS):
            lo = (g + 1) * LANES
            for hd in range(HA):
                bias_ref[hd // 2, (hd % 2) * t:(hd % 2 + 1) * t, g * LANES:(g + 1) * LANES] = (
                    LOG2E * _toeplitz(rel_ref[hd:hd + 1, lo:lo + LANES],
                                      rel_ref[hd:hd + 1, lo - LANES:lo], t, 0))

    lo, hi, is_lo = _pair_masks()
    items = [(bi, pair) for bi in range(q_ref.shape[0]) for pair in range(HA // 2)]

    def scores(bi, pair):
        ls = slice(pair * LANES, (pair + 1) * LANES)
        q = q_ref[bi, :, ls]
        q2 = jnp.concatenate([q * lo, q * hi], axis=0)
        kct = kct_ref[bi, 2 * pair:2 * pair + 2].reshape(LANES, BAND_PREV).astype(BF16)
        return (_dot(q2, kct) + bias_ref[pair, :, :BAND_PREV],
                _dot_nt(q2, kn_ref[bi, :, ls]) + bias_ref[pair, :, BAND_PREV:BAND_PREV + t])

    def finish(bi, pair, sc, sn):
        ls = slice(pair * LANES, (pair + 1) * LANES)
        vct = vct_ref[bi, 2 * pair:2 * pair + 2].reshape(LANES, BAND_PREV).astype(BF16)
        o2 = _softmax_pv([sc, sn], [lambda p: _dot_nt(p, vct), lambda p: _dot(p, vn_ref[bi, :, ls])])
        o_ref[bi, :, ls] = jnp.where(is_lo, o2[:t], o2[t:]).astype(o_ref.dtype)

    pending = [scores(*item) for item in items[:A_LOOKAHEAD]]
    for n, item in enumerate(items):
        if n + A_LOOKAHEAD < len(items):
            pending.append(scores(*items[n + A_LOOKAHEAD]))
        finish(*item, *pending.pop(0))


def _band_sample(q, kn, vn, cache_kt, cache_vt, layer, rel, group):
    b, t, w = q.shape
    a_len = cache_kt.shape[4]
    assert b % group == 0 and a_len == BAND_PREV and t == CHUNK
    new = pl.BlockSpec((group, t, w), lambda i: (i, 0, 0))
    old = pl.BlockSpec((None, group, HA, DA, a_len), lambda i: (layer, i, 0, 0, 0))
    return pl.pallas_call(
        _band_sample_kernel,
        grid=(b // group,),
        in_specs=[new, old, old, new, new, _resident(rel.shape)],
        out_specs=new,
        out_shape=jax.ShapeDtypeStruct((b, t, w), BF16),
        scratch_shapes=[pltpu.VMEM((HA // 2, 2 * t, A_WIN), F32)],
        compiler_params=_cparams("arbitrary"),
        name="band_sample",
    )(q, cache_kt, cache_vt, kn, vn, rel)


def _lam(lq1_ref, lk1_ref, lq2_ref, lk2_ref, lam_init):
    s1 = jnp.sum(lq1_ref[...] * lk1_ref[...], axis=-1, keepdims=True)
    s2 = jnp.sum(lq2_ref[...] * lk2_ref[...], axis=-1, keepdims=True)
    return jnp.exp(s1) - jnp.exp(s2) + lam_init


def _diff_prompt_kernel(q_ref, kt_ref, v_ref, lq1_ref, lk1_ref, lq2_ref, lk2_ref, gsub_ref,
                        o_ref, diag_ref, *, lam_init, tiles):
    i = pl.program_id(1)
    lam = _lam(lq1_ref, lk1_ref, lq2_ref, lk2_ref, lam_init)
    lo, hi, _ = _pair_masks()

    @pl.when((pl.program_id(0) == 0) & (i == 0))
    def _():
        qc = (lax.broadcasted_iota(jnp.int32, (2 * Q_TILE, Q_TILE), 0) % Q_TILE) // CHUNK
        kc = lax.broadcasted_iota(jnp.int32, (2 * Q_TILE, Q_TILE), 1) // CHUNK
        diag_ref[...] = jnp.where(kc <= qc, 0.0, NEG).astype(F32)

    def scores(t, hd):
        ls = slice(hd * EB, (hd + 1) * EB)
        q = q_ref[:, ls]
        q12 = jnp.concatenate([q * lo, q * hi], axis=0)
        return _dot(q12, kt_ref[ls, :(t + 1) * Q_TILE])

    def finish(t, hd, s):
        ls = slice(hd * EB, (hd + 1) * EB)
        past = t * Q_TILE
        s_diag = s[:, past:] + diag_ref[...]
        m = jnp.max(s_diag, axis=-1, keepdims=True)
        if t > 0:
            m = jnp.maximum(m, jnp.max(s[:, :past], axis=-1, keepdims=True))
        p = jnp.exp2(s_diag - m)
        if t > 0:
            p = jnp.concatenate([jnp.exp2(s[:, :past] - m), p], axis=1)
        l = jnp.sum(p, axis=-1, keepdims=True)
        o12 = _dot(p.astype(BF16), v_ref[:past + Q_TILE, ls]) * (1.0 / l)
        o = o12[:Q_TILE] - lam * o12[Q_TILE:]
        o_ref[:, ls] = (_rms(o, gsub_ref[...]) * (1.0 - lam_init)).astype(o_ref.dtype)

    def tile(t):
        pending = [scores(t, hd) for hd in range(B_LOOKAHEAD)]
        for hd in range(HB):
            if hd + B_LOOKAHEAD < HB:
                pending.append(scores(t, hd + B_LOOKAHEAD))
            finish(t, hd, pending.pop(0))

    for t in range(tiles):
        pl.when(i == t)(functools.partial(tile, t))


def _diff_prompt(q, kt, v, lams, gsub, lam_init, bsz):
    n, w = q.shape
    s = n // bsz
    tiles = s // Q_TILE
    qspec = pl.BlockSpec((Q_TILE, w), lambda b, i: (b * tiles + i, 0))
    kfull = pl.BlockSpec((w, s), lambda b, i: (0, b))
    vfull = pl.BlockSpec((s, w), lambda b, i: (b, 0))
    vec = _resident((1, DB))
    return pl.pallas_call(
        functools.partial(_diff_prompt_kernel, lam_init=lam_init, tiles=tiles),
        grid=(bsz, tiles),
        in_specs=[qspec, kfull, vfull, vec, vec, vec, vec, _resident((1, EB))],
        out_specs=qspec,
        out_shape=jax.ShapeDtypeStruct((n, w), BF16),
        scratch_shapes=[pltpu.VMEM((2 * Q_TILE, Q_TILE), F32)],
        compiler_params=_cparams("arbitrary", "arbitrary"),
        name="diff_prompt",
    )(q, kt, v, *lams, gsub)


def _diff_sample_kernel(q_ref, kc_ref, vc_ref, kn_ref, vn_ref, lq1_ref, lk1_ref, lq2_ref,
                        lk2_ref, gsub_ref, o_ref, *, lam_init):
    lam = _lam(lq1_ref, lk1_ref, lq2_ref, lk2_ref, lam_init)
    lo, hi, _ = _pair_masks()
    past = kc_ref.shape[0] // HB
    t = q_ref.shape[1]

    def scores(hd):
        ls = slice(hd * EB, (hd + 1) * EB)
        q = q_ref[0, :, ls]
        q12 = jnp.concatenate([q * lo, q * hi], axis=0)
        kc = kc_ref[pl.ds(hd, past, stride=HB), :].astype(BF16)
        return _dot_nt(q12, kc), _dot_nt(q12, kn_ref[0, :, ls])

    def finish(hd, sc, sn):
        ls = slice(hd * EB, (hd + 1) * EB)
        vc = vc_ref[pl.ds(hd, past, stride=HB), :].astype(BF16)
        o12 = _softmax_pv([sc, sn], [lambda p: _dot(p, vc), lambda p: _dot(p, vn_ref[0, :, ls])])
        o = o12[:t] - lam * o12[t:]
        o_ref[0, :, ls] = (_rms(o, gsub_ref[...]) * (1.0 - lam_init)).astype(o_ref.dtype)

    pending = [scores(hd) for hd in range(B_LOOKAHEAD)]
    for hd in range(HB):
        if hd + B_LOOKAHEAD < HB:
            pending.append(scores(hd + B_LOOKAHEAD))
        finish(hd, *pending.pop(0))


def _diff_sample(q, kn, vn, cache_k, cache_v, layer, lams, gsub, lam_init):
    b, t, w = q.shape
    rows = cache_k.shape[2]
    new = pl.BlockSpec((1, t, w), lambda i: (i, 0, 0))
    old = pl.BlockSpec((None, None, rows, EB), lambda i: (layer, i, 0, 0))
    vec = _resident((1, DB))
    return pl.pallas_call(
        functools.partial(_diff_sample_kernel, lam_init=lam_init),
        grid=(b,),
        in_specs=[new, old, old, new, new, vec, vec, vec, vec, _resident((1, EB))],
        out_specs=new,
        out_shape=jax.ShapeDtypeStruct((b, t, w), BF16),
        compiler_params=_cparams("parallel"),
        name="diff_sample",
    )(q, cache_k, cache_v, kn, vn, *lams, gsub)


def _rope_tables(pos):
    half = DB // 2
    inv = ROPE_THETA ** (-jnp.arange(half, dtype=F32) * 2.0 / DB)
    ang = pos.astype(F32)[:, None] * inv[None, :]
    cos, sin = jnp.cos(ang), jnp.sin(ang)
    cos2 = jnp.concatenate([cos, cos], axis=-1)
    sin2 = jnp.concatenate([-sin, sin], axis=-1)
    return jnp.tile(cos2, (1, LANES // DB)), jnp.tile(sin2, (1, LANES // DB))


def _rel_rows(table):
    table = table.astype(F32)
    n_const = REL_SPAN - (2 * MAX_REL - 1)
    const = jnp.broadcast_to(table[:, 2 * MAX_REL:], (HA, n_const))
    desc = jnp.concatenate([table[:, 1:2 * MAX_REL], const], axis=1)
    asc = jnp.concatenate([const, table[:, 1:2 * MAX_REL][:, ::-1]], axis=1)
    return asc, desc


def kernel(x_prompt, x_sample, cache_a_k, cache_a_v, cache_b_k, cache_b_v,
           g_ffn1, w1_gate, w1_up, w1_down, g_mix, w_in, g_qa, g_ka, g_qb, g_kb,
           rel_bias, lam_q1, lam_k1, lam_q2, lam_k2, g_sub, w_out,
           g_ffn2, w2_gate, w2_up, w2_down):
    bp, s, d = x_prompt.shape
    bd, t, _ = x_sample.shape
    depth = w_in.shape[0]
    past = cache_b_k.shape[2]
    a_len = cache_a_k.shape[2]
    assert a_len == BAND_PREV and t == CHUNK and s % Q_TILE == 0 and MAX_REL == LANES
    keep_p = min(BAND_PREV, s)

    cos_p, sin_p = _rope_tables(jnp.arange(s))
    reps = TOKEN_TILE // t
    cos_s, sin_s = (jnp.tile(a, (reps, 1)) for a in _rope_tables(past + jnp.arange(t)))

    cakt = jnp.transpose(cache_a_k, (0, 1, 3, 4, 2))
    cavt = jnp.transpose(cache_a_v, (0, 1, 3, 4, 2))
    cbk = cache_b_k.reshape(depth, bd, past * HB, EB)
    cbv = cache_b_v.reshape(depth, bd, past * HB, EB)

    yp = x_prompt.reshape(bp * s, d)
    ys = x_sample.reshape(bd * t, d)
    pb = sb = None
    w1 = (w1_gate.astype(BF16), w1_up.astype(BF16), w1_down.astype(BF16))
    w2 = (w2_gate.astype(BF16), w2_up.astype(BF16), w2_down.astype(BF16))
    wi = w_in.astype(BF16)
    wo = w_out.astype(BF16)
    for l in range(depth):
        lam_init = 0.8 - 0.6 * math.exp(-0.3 * l)
        row = lambda a: a[l].astype(F32).reshape(1, -1)
        tiled = lambda a: jnp.tile(row(a), (1, GROUP_W // a.shape[1]))
        gains = (tiled(g_qa), tiled(g_ka), tiled(g_qb), tiled(g_kb))
        lams = (row(lam_q1), row(lam_k1), row(lam_q2), row(lam_k2))
        gsub = row(g_sub)
        rel_asc, rel_desc = _rel_rows(rel_bias[l])

        yp = _ffn(yp, row(g_ffn1), *w1, l)
        qat, ka, vat, qb, kbt, vb, *pb = _proj(
            yp, row(g_mix), wi, *gains, cos_p, sin_p, s, l, depth, True, pb)
        oa = _band_prompt(qat, ka, vat, rel_desc, bp)
        ob = _diff_prompt(qb, kbt, vb, lams, gsub, lam_init, bp)
        yp = _out_ffn(yp, oa, ob, wo, row(g_ffn2), *w2, l)

        ys = _ffn(ys, row(g_ffn1), *w1, l)
        qa, ka, va, qb, kb, vb, *sb = _proj(
            ys, row(g_mix), wi, *gains, cos_s, sin_s, t, l, depth, False, sb)
        sh = lambda a: a.reshape(bd, t, GROUP_W)
        oa = _band_sample(sh(qa), sh(ka), sh(va), cakt, cavt, l, rel_asc, 4)
        ob = _diff_sample(sh(qb), sh(kb), sh(vb), cbk, cbv, l, lams, gsub, lam_init)
        ys = _out_ffn(ys, oa.reshape(bd * t, GROUP_W), ob.reshape(bd * t, GROUP_W),
                      wo, row(g_ffn2), *w2, l)

    def kept_prompt(a):
        return jnp.transpose(a.reshape(depth, bp, HA, DA, keep_p), (0, 1, 4, 2, 3))

    return (yp.reshape(bp, s, d), ys.reshape(bd, t, d),
            kept_prompt(pb[0]), kept_prompt(pb[1]),
            pb[2].reshape(depth, bp, s, HB, EB), pb[3].reshape(depth, bp, s, HB, EB),
            sb[0].reshape(depth, bd, t, HA, DA), sb[1].reshape(depth, bd, t, HA, DA),
            sb[2].reshape(depth, bd, t, HB, EB), sb[3].reshape(depth, bd, t, HB, EB))
```

```python
import functools
import math

import jax
import jax.numpy as jnp
from jax import lax
from jax.experimental import pallas as pl
from jax.experimental.pallas import tpu as pltpu

CHUNK = 64
PREV_CHUNKS = 8
BAND_PREV = PREV_CHUNKS * CHUNK
MAX_REL = 128
HA = 8
DA = 64
HB = 4
DB = 64
EB = 2 * DB
GROUP_W = HA * DA
ROPE_THETA = 10000.0
EPS = 1e-6
NEG = -1e30
LOG2E = math.log2(math.e)

LANES = 128
MXU_DIM = 256
TOKEN_TILE = 512
Q_TILE = 256
A_WIN = BAND_PREV + LANES
A_GROUPS = A_WIN // LANES
REL_SPAN = A_WIN + LANES
A_LOOKAHEAD = 3
B_LOOKAHEAD = 2
B_TILES_PER_STEP = 2
VMEM_LIMIT = 56 * 1024 * 1024

F32 = jnp.float32
BF16 = jnp.bfloat16


def _cparams(*sem):
    return pltpu.CompilerParams(dimension_semantics=sem, vmem_limit_bytes=VMEM_LIMIT)


def _resident(shape):
    nd = len(shape)
    return pl.BlockSpec(shape, lambda *_: (0,) * nd, pipeline_mode=pl.Buffered(1))


def _layer_resident(shape, layer, row_block=0):
    return pl.BlockSpec((None,) + tuple(shape), lambda *_: (layer, row_block, 0),
                        pipeline_mode=pl.Buffered(1))


def _dot(a, b):
    return jnp.dot(a, b, preferred_element_type=F32)


def _dot_nt(a, b):
    return lax.dot_general(a, b, (((1,), (1,)), ((), ())), preferred_element_type=F32)


def _rms(x, g):
    return x * lax.rsqrt(jnp.mean(x * x, axis=-1, keepdims=True) + EPS) * g


def _swiglu_half(x, g, wg_ref, wu_ref, wd_ref):
    h = _rms(x, g).astype(BF16)
    d_ff = wg_ref.shape[1]
    split = min(d_ff, pl.cdiv(d_ff, 2 * MXU_DIM) * MXU_DIM)
    y = None
    for lo, hi in ((0, split), (split, d_ff)):
        if lo == hi:
            continue
        gate = _dot(h, wg_ref[:, lo:hi])
        up = _dot(h, wu_ref[:, lo:hi])
        a = (gate / (1.0 + jnp.exp(-gate)) * up).astype(BF16)
        part = _dot(a, wd_ref[lo:hi, :])
        y = part if y is None else y + part
    return x + 0.5 * y


def _ffn_kernel(x_ref, g_ref, wg_ref, wu_ref, wd_ref, o_ref):
    o_ref[...] = _swiglu_half(x_ref[...], g_ref[...], wg_ref, wu_ref, wd_ref)


def _out_ffn_kernel(x_ref, oa_ref, ob_ref, woa_ref, wob_ref, g_ref, wg_ref, wu_ref, wd_ref,
                    o_ref):
    x = x_ref[...] + _dot(oa_ref[...], woa_ref[...]) + _dot(ob_ref[...], wob_ref[...])
    o_ref[...] = _swiglu_half(x, g_ref[...], wg_ref, wu_ref, wd_ref)


def _ffn(x, g, wg, wu, wd, layer):
    n, d = x.shape
    d_ff = wg.shape[2]
    assert n % TOKEN_TILE == 0
    row = pl.BlockSpec((TOKEN_TILE, d), lambda i: (i, 0))
    return pl.pallas_call(
        _ffn_kernel,
        grid=(n // TOKEN_TILE,),
        in_specs=[row, _resident((1, d)), _layer_resident((d, d_ff), layer),
                  _layer_resident((d, d_ff), layer), _layer_resident((d_ff, d), layer)],
        out_specs=row,
        out_shape=jax.ShapeDtypeStruct((n, d), F32),
        compiler_params=_cparams("parallel"),
        name="ffn",
    )(x, g, wg, wu, wd)


def _out_ffn(x, oa, ob, wo, g, wg, wu, wd, layer):
    n, d = x.shape
    d_ff = wg.shape[2]
    row = pl.BlockSpec((TOKEN_TILE, d), lambda i: (i, 0))
    grp = pl.BlockSpec((TOKEN_TILE, GROUP_W), lambda i: (i, 0))
    return pl.pallas_call(
        _out_ffn_kernel,
        grid=(n // TOKEN_TILE,),
        in_specs=[row, grp, grp, _layer_resident((GROUP_W, d), layer, 0),
                  _layer_resident((GROUP_W, d), layer, 1), _resident((1, d)),
                  _layer_resident((d, d_ff), layer), _layer_resident((d, d_ff), layer),
                  _layer_resident((d_ff, d), layer)],
        out_specs=row,
        out_shape=jax.ShapeDtypeStruct((n, d), F32),
        compiler_params=_cparams("parallel"),
        name="out_ffn",
    )(x, oa, ob, wo, wo, g, wg, wu, wd)


def _head_norm(z, gain, ones_bd):
    outs = []
    for c in range(0, z.shape[1], MXU_DIM):
        zc = z[:, c:c + MXU_DIM]
        ss = _dot((zc * zc).astype(BF16), ones_bd)
        outs.append(zc * lax.rsqrt(ss * (1.0 / DA) + EPS) * gain[:, c:c + MXU_DIM])
    return jnp.concatenate(outs, axis=1)


def _rope(y, cos2, sin2):
    first_half = (lax.broadcasted_iota(jnp.int32, (1, LANES), 1) % DB) < (DB // 2)
    outs = []
    for c in range(0, y.shape[1], LANES):
        yc = y[:, c:c + LANES]
        swapped = jnp.where(first_half, pltpu.roll(yc, LANES - DB // 2, 1),
                            pltpu.roll(yc, DB // 2, 1))
        outs.append(yc * cos2 + swapped * sin2)
    return jnp.concatenate(outs, axis=1)


def _store_heads_interleaved(o_ref, slot, val):
    m = val.shape[0]
    for s in range(o_ref.shape[0]):
        if s == slot:
            for hd in range(HB):
                o_ref[s, pl.ds(hd, m, stride=HB), :] = val[:, hd * EB:(hd + 1) * EB]
        else:
            o_ref[s] = jnp.zeros(o_ref.shape[1:], o_ref.dtype)


def _proj_kernel(*refs, tiles_per_seq, transposed, aliased, slot):
    (x_ref, g_ref, w_ref, gqa_ref, gka_ref, gqb_ref, gkb_ref, cos_ref, sin_ref) = refs[:9]
    outs = refs[9 + (4 if aliased else 0):]
    qa_o, ka_o, va_o, qb_o, kb_o, vb_o, ka32_o, va32_o, kb32_o, vb32_o = outs
    h = _rms(x_ref[...], g_ref[...]).astype(BF16)
    r = lax.broadcasted_iota(jnp.int32, (MXU_DIM, MXU_DIM), 0) // DA
    c = lax.broadcasted_iota(jnp.int32, (MXU_DIM, MXU_DIM), 1) // DA
    ones_bd = (r == c).astype(BF16)
    cos2 = cos_ref[...]
    sin2 = sin_ref[...]

    def group(j):
        return _dot(h, w_ref[:, j * GROUP_W:(j + 1) * GROUP_W])

    def maybe_t(a):
        return a.T if transposed else a

    gqa = gqa_ref[...] * (DA ** -0.5 * LOG2E)
    gqb = gqb_ref[...] * (DB ** -0.5 * LOG2E)
    kept = []

    def finish_qa(z):
        qa_o[...] = maybe_t(_head_norm(z, gqa, ones_bd)).astype(BF16)

    def finish_ka(z):
        ka = _head_norm(z, gka_ref[...], ones_bd)
        ka_o[...] = ka.astype(BF16)
        kept.append((ka32_o, maybe_t(ka)))

    def finish_va(z):
        va_t = maybe_t(z)
        va_o[...] = va_t.astype(BF16)
        kept.append((va32_o, va_t))

    def finish_qb(z):
        qb_o[...] = _rope(_head_norm(z, gqb, ones_bd), cos2, sin2).astype(BF16)

    def finish_kb(z):
        kb = _rope(_head_norm(z, gkb_ref[...], ones_bd), cos2, sin2)
        kb_o[...] = maybe_t(kb).astype(BF16)
        _store_heads_interleaved(kb32_o, slot, kb)

    def finish_vb(z):
        vb_o[...] = z.astype(BF16)
        _store_heads_interleaved(vb32_o, slot, z)

    order = ((3, finish_qb), (4, finish_kb), (0, finish_qa), (1, finish_ka), (5, finish_vb),
             (2, finish_va))
    z = group(order[0][0])
    for n, (_, finish) in enumerate(order):
        nxt = group(order[n + 1][0]) if n + 1 < len(order) else None
        finish(z)
        z = nxt

    @pl.when(pl.program_id(0) % tiles_per_seq == tiles_per_seq - 1)
    def _():
        for o_ref, val in kept:
            for s in range(o_ref.shape[0]):
                o_ref[s] = val if s == slot else jnp.zeros_like(val)


def _proj(x, g, w_in, gqa, gka, gqb, gkb, cos2, sin2, seq, layer, depth, transposed, prev):
    n, d = x.shape
    keep = min(BAND_PREV, seq)
    if seq >= TOKEN_TILE:
        assert seq % TOKEN_TILE == 0 and keep == TOKEN_TILE
        tiles_per_seq = seq // TOKEN_TILE
    else:
        assert TOKEN_TILE % seq == 0 and keep == seq and not transposed
        tiles_per_seq = 1
    n_seq = n // seq
    aliased = prev is not None
    slots, first_slot, slot = (1, layer, 0) if aliased else (depth, 0, layer)
    pos_tiles = cos2.shape[0] // TOKEN_TILE
    row = pl.BlockSpec((TOKEN_TILE, d), lambda i: (i, 0))
    grp = pl.BlockSpec((TOKEN_TILE, GROUP_W), lambda i: (i, 0))
    grp_t = pl.BlockSpec((GROUP_W, TOKEN_TILE), lambda i: (0, i))
    tab = pl.BlockSpec((TOKEN_TILE, LANES), lambda i: (i % pos_tiles, 0))
    gain = _resident((1, GROUP_W))
    nat = jax.ShapeDtypeStruct((n, GROUP_W), BF16)
    tr = jax.ShapeDtypeStruct((GROUP_W, n), BF16)
    if transposed:
        qv_spec, qv_shape = grp_t, tr
        kept = pl.BlockSpec((slots, None, GROUP_W, keep),
                            lambda i: (first_slot, i // tiles_per_seq, 0, 0))
        kept_shape = jax.ShapeDtypeStruct((depth, n_seq, GROUP_W, keep), F32)
    else:
        qv_spec, qv_shape = grp, nat
        kept = pl.BlockSpec((slots, TOKEN_TILE, GROUP_W),
                            lambda i: (first_slot, i // tiles_per_seq, 0))
        kept_shape = jax.ShapeDtypeStruct((depth, n_seq * keep, GROUP_W), F32)
    cache_b = pl.BlockSpec((slots, TOKEN_TILE * HB, EB), lambda i: (first_slot, i, 0))
    cache_b_shape = jax.ShapeDtypeStruct((depth, n * HB, EB), F32)
    any_spec = pl.BlockSpec(memory_space=pl.ANY)
    return pl.pallas_call(
        functools.partial(_proj_kernel, tiles_per_seq=tiles_per_seq, transposed=transposed,
                          aliased=aliased, slot=slot),
        grid=(n // TOKEN_TILE,),
        in_specs=[row, _resident((1, d)), _layer_resident(w_in.shape[1:], layer), gain, gain, gain,
                  gain, tab, tab]
                 + ([any_spec] * 4 if aliased else []),
        out_specs=[qv_spec, grp, qv_spec, grp, qv_spec, grp, kept, kept, cache_b, cache_b],
        out_shape=[qv_shape, nat, qv_shape, nat, qv_shape, nat, kept_shape, kept_shape,
                   cache_b_shape, cache_b_shape],
        input_output_aliases={9: 6, 10: 7, 11: 8, 12: 9} if aliased else {},
        compiler_params=_cparams("arbitrary"),
        name="proj",
    )(x, g, w_in, gqa, gka, gqb, gkb, cos2, sin2, *(prev if aliased else ()))


def _row_half_masks(cols=LANES):
    row = lax.broadcasted_iota(jnp.int32, (LANES, cols), 0)
    top = jnp.where(row < LANES // 2, 1.0, 0.0).astype(BF16)
    bottom = jnp.where(row >= LANES // 2, 1.0, 0.0).astype(BF16)
    return top, bottom


def _toeplitz(tab_c, tab_prev, rows, base_shift):
    lane = lax.broadcasted_iota(jnp.int32, (rows, LANES), 1)
    shift = lax.broadcasted_iota(jnp.int32, (rows, LANES), 0) + base_shift
    cur = pltpu.roll(jnp.broadcast_to(tab_c, (rows, LANES)), base_shift, 1, stride=1, stride_axis=0)
    prev = pltpu.roll(jnp.broadcast_to(tab_prev, (rows, LANES)), base_shift, 1, stride=1,
                      stride_axis=0)
    return jnp.where(lane >= shift, cur, prev)


def _band_prompt_kernel(qt_ref, k0_ref, k1_ref, k2_ref, v0_ref, v1_ref, v2_ref, rel_ref, o_ref,
                        bias_ref):
    i = pl.program_id(1)

    @pl.when((pl.program_id(0) == 0) & (i == 0))
    def _():
        kc = lax.broadcasted_iota(jnp.int32, (LANES, LANES), 0) // CHUNK
        qc = lax.broadcasted_iota(jnp.int32, (LANES, LANES), 1) // CHUNK
        for g in range(A_GROUPS):
            dc = kc + (LANES // CHUNK) * g - qc
            visible = (dc >= 0) & (dc <= PREV_CHUNKS)
            hi = (A_GROUPS - 1 - g) * LANES
            for hd in range(HA):
                t = _toeplitz(rel_ref[hd:hd + 1, hi + LANES:hi + 2 * LANES],
                              rel_ref[hd:hd + 1, hi:hi + LANES], LANES, 1)
                bias_ref[hd, g * LANES:(g + 1) * LANES, :] = jnp.where(visible, t * LOG2E, NEG)
        bias_ref[HA] = jnp.full((A_WIN, LANES), NEG, F32)

    top, bottom = _row_half_masks()
    k_refs = (k0_ref, k1_ref, k2_ref)
    v_refs = (v0_ref, v1_ref, v2_ref)
    halves = Q_TILE // LANES
    items = [(pair, half) for pair in range(HA // 2) for half in range(halves)]

    def scores(pair, half):
        ls = slice(pair * LANES, (pair + 1) * LANES)
        kwin = jnp.concatenate([r[:, ls] for r in k_refs], axis=0)[half * LANES:half * LANES + A_WIN]
        qt = qt_ref[ls, half * LANES:(half + 1) * LANES]
        return _dot(kwin, jnp.concatenate([qt * top, qt * bottom], axis=1))

    def finish(pair, half, st2):
        ls = slice(pair * LANES, (pair + 1) * LANES)
        ps, sums = [], []
        for sub in range(2):
            hd = 2 * pair + sub
            parts = []
            for g in range(A_GROUPS):
                idx = jnp.where(2 * i + half + g >= A_GROUPS - 1, hd, HA)
                parts.append(st2[g * LANES:(g + 1) * LANES, sub * LANES:(sub + 1) * LANES]
                             + bias_ref[idx, g * LANES:(g + 1) * LANES, :])
            s = jnp.concatenate(parts, axis=0)
            p = jnp.exp2(s - jnp.max(s, axis=0, keepdims=True))
            sums.append(jnp.sum(p, axis=0, keepdims=True))
            ps.append(p.astype(BF16))
        vwin = jnp.concatenate([r[ls, :] for r in v_refs], axis=1)[:, half * LANES:half * LANES + A_WIN]
        ot2 = _dot(vwin, jnp.concatenate(ps, axis=1))
        ot = jnp.concatenate([ot2[:DA, :LANES] * (1.0 / sums[0]),
                              ot2[DA:, LANES:] * (1.0 / sums[1])], axis=0)
        o_ref[half * LANES:(half + 1) * LANES, ls] = ot.T.astype(o_ref.dtype)

    pending = [scores(*item) for item in items[:A_LOOKAHEAD]]
    for n, item in enumerate(items):
        if n + A_LOOKAHEAD < len(items):
            pending.append(scores(*items[n + A_LOOKAHEAD]))
        finish(*item, pending.pop(0))


def _band_prompt(qt, k, vt, rel, bsz):
    w, n = qt.shape
    tiles = n // bsz // Q_TILE
    qspec = pl.BlockSpec((w, Q_TILE), lambda b, i: (0, b * tiles + i))
    ospec = pl.BlockSpec((Q_TILE, w), lambda b, i: (b * tiles + i, 0))

    def kspec(back):
        return pl.BlockSpec((Q_TILE, w), lambda b, i: (b * tiles + jnp.maximum(i - back, 0), 0))

    def vspec(back):
        return pl.BlockSpec((w, Q_TILE), lambda b, i: (0, b * tiles + jnp.maximum(i - back, 0)))

    return pl.pallas_call(
        _band_prompt_kernel,
        grid=(bsz, tiles),
        in_specs=[qspec, kspec(2), kspec(1), kspec(0), vspec(2), vspec(1), vspec(0),
                  _resident(rel.shape)],
        out_specs=ospec,
        out_shape=jax.ShapeDtypeStruct((n, w), BF16),
        scratch_shapes=[pltpu.VMEM((HA + 1, A_WIN, LANES), F32)],
        compiler_params=_cparams("arbitrary", "arbitrary"),
        name="band_prompt",
    )(qt, k, k, k, vt, vt, vt, rel)


def _pair_masks():
    lane = lax.broadcasted_iota(jnp.int32, (1, LANES), 1)
    lo = jnp.where(lane < DA, 1.0, 0.0).astype(BF16)
    hi = jnp.where(lane >= DA, 1.0, 0.0).astype(BF16)
    return lo, hi, lane < DA


def _softmax_pv(s_list, pv_list):
    m = None
    for s in s_list:
        mx = jnp.max(s, axis=-1, keepdims=True)
        m = mx if m is None else jnp.maximum(m, mx)
    l = None
    o = None
    for s, pv in zip(s_list, pv_list):
        p = jnp.exp2(s - m)
        ps = jnp.sum(p, axis=-1, keepdims=True)
        po = pv(p.astype(BF16))
        l = ps if l is None else l + ps
        o = po if o is None else o + po
    return o * (1.0 / l)


def _band_sample_kernel(q_ref, kct_ref, vct_ref, kn_ref, vn_ref, rel_ref, o_ref, bias_ref):
    t = q_ref.shape[1]

    @pl.when(pl.program_id(0) == 0)
    def _():
        for g in range(A_GROUPS):
            lo = (g + 1) * LANES
            for hd in range(HA):
                bias_ref[hd // 2, (hd % 2) * t:(hd % 2 + 1) * t, g * LANES:(g + 1) * LANES] = (
                    LOG2E * _toeplitz(rel_ref[hd:hd + 1, lo:lo + LANES],
                                      rel_ref[hd:hd + 1, lo - LANES:lo], t, 0))

    lo, hi, is_lo = _pair_masks()
    items = [(bi, pair) for bi in range(q_ref.shape[0]) for pair in range(HA // 2)]

    def scores(bi, pair):
        ls = slice(pair * LANES, (pair + 1) * LANES)
        q = q_ref[bi, :, ls]
        q2 = jnp.concatenate([q * lo, q * hi], axis=0)
        kct = kct_ref[bi, 2 * pair:2 * pair + 2].reshape(LANES, BAND_PREV).astype(BF16)
        return (_dot(q2, kct) + bias_ref[pair, :, :BAND_PREV],
                _dot_nt(q2, kn_ref[bi, :, ls]) + bias_ref[pair, :, BAND_PREV:BAND_PREV + t])

    def finish(bi, pair, sc, sn):
        ls = slice(pair * LANES, (pair + 1) * LANES)
        vct = vct_ref[bi, 2 * pair:2 * pair + 2].reshape(LANES, BAND_PREV).astype(BF16)
        o2 = _softmax_pv([sc, sn], [lambda p: _dot_nt(p, vct), lambda p: _dot(p, vn_ref[bi, :, ls])])
        o_ref[bi, :, ls] = jnp.where(is_lo, o2[:t], o2[t:]).astype(o_ref.dtype)

    pending = [scores(*item) for item in items[:A_LOOKAHEAD]]
    for n, item in enumerate(items):
        if n + A_LOOKAHEAD < len(items):
            pending.append(scores(*items[n + A_LOOKAHEAD]))
        finish(*item, *pending.pop(0))


def _band_sample(q, kn, vn, cache_kt, cache_vt, layer, rel, group):
    b, t, w = q.shape
    a_len = cache_kt.shape[4]
    assert b % group == 0 and a_len == BAND_PREV and t == CHUNK
    new = pl.BlockSpec((group, t, w), lambda i: (i, 0, 0))
    old = pl.BlockSpec((None, group, HA, DA, a_len), lambda i: (layer, i, 0, 0, 0))
    return pl.pallas_call(
        _band_sample_kernel,
        grid=(b // group,),
        in_specs=[new, old, old, new, new, _resident(rel.shape)],
        out_specs=new,
        out_shape=jax.ShapeDtypeStruct((b, t, w), BF16),
        scratch_shapes=[pltpu.VMEM((HA // 2, 2 * t, A_WIN), F32)],
        compiler_params=_cparams("arbitrary"),
        name="band_sample",
    )(q, cache_kt, cache_vt, kn, vn, rel)


def _lam(lq1_ref, lk1_ref, lq2_ref, lk2_ref, lam_init):
    s1 = jnp.sum(lq1_ref[...] * lk1_ref[...], axis=-1, keepdims=True)
    s2 = jnp.sum(lq2_ref[...] * lk2_ref[...], axis=-1, keepdims=True)
    return jnp.exp(s1) - jnp.exp(s2) + lam_init


def _diff_prompt_kernel(q_ref, kt_ref, v_ref, lq1_ref, lk1_ref, lq2_ref, lk2_ref, gsub_ref,
                        o_ref, diag_ref, *, lam_init, steps):
    i = pl.program_id(1)
    lam = _lam(lq1_ref, lk1_ref, lq2_ref, lk2_ref, lam_init)
    lo, hi, _ = _pair_masks()

    @pl.when((pl.program_id(0) == 0) & (i == 0))
    def _():
        qc = (lax.broadcasted_iota(jnp.int32, (2 * Q_TILE, Q_TILE), 0) % Q_TILE) // CHUNK
        kc = lax.broadcasted_iota(jnp.int32, (2 * Q_TILE, Q_TILE), 1) // CHUNK
        diag_ref[...] = jnp.where(kc <= qc, 0.0, NEG).astype(F32)

    def scores(t, u, hd):
        ls = slice(hd * EB, (hd + 1) * EB)
        q = q_ref[u * Q_TILE:(u + 1) * Q_TILE, ls]
        q12 = jnp.concatenate([q * lo, q * hi], axis=0)
        return _dot(q12, kt_ref[ls, :(t + 1) * Q_TILE])

    def finish(t, u, hd, s):
        ls = slice(hd * EB, (hd + 1) * EB)
        past = t * Q_TILE
        s_diag = s[:, past:] + diag_ref[...]
        m = jnp.max(s_diag, axis=-1, keepdims=True)
        if t > 0:
            m = jnp.maximum(m, jnp.max(s[:, :past], axis=-1, keepdims=True))
        p = jnp.exp2(s_diag - m)
        if t > 0:
            p = jnp.concatenate([jnp.exp2(s[:, :past] - m), p], axis=1)
        l = jnp.sum(p, axis=-1, keepdims=True)
        o12 = _dot(p.astype(BF16), v_ref[:past + Q_TILE, ls]) * (1.0 / l)
        o = o12[:Q_TILE] - lam * o12[Q_TILE:]
        o_ref[u * Q_TILE:(u + 1) * Q_TILE, ls] = (
            (_rms(o, gsub_ref[...]) * (1.0 - lam_init)).astype(o_ref.dtype))

    def program(step):
        items = [(step * B_TILES_PER_STEP + u, u, hd)
                 for u in range(B_TILES_PER_STEP) for hd in range(HB)]
        pending = [scores(*item) for item in items[:B_LOOKAHEAD]]
        for n, item in enumerate(items):
            if n + B_LOOKAHEAD < len(items):
                pending.append(scores(*items[n + B_LOOKAHEAD]))
            finish(*item, pending.pop(0))

    for step in range(steps):
        pl.when(i == step)(functools.partial(program, step))


def _diff_prompt(q, kt, v, lams, gsub, lam_init, bsz):
    n, w = q.shape
    s = n // bsz
    rows = B_TILES_PER_STEP * Q_TILE
    assert s % rows == 0
    steps = s // rows
    qspec = pl.BlockSpec((rows, w), lambda b, i: (b * steps + i, 0))
    kfull = pl.BlockSpec((w, s), lambda b, i: (0, b))
    vfull = pl.BlockSpec((s, w), lambda b, i: (b, 0))
    vec = _resident((1, DB))
    return pl.pallas_call(
        functools.partial(_diff_prompt_kernel, lam_init=lam_init, steps=steps),
        grid=(bsz, steps),
        in_specs=[qspec, kfull, vfull, vec, vec, vec, vec, _resident((1, EB))],
        out_specs=qspec,
        out_shape=jax.ShapeDtypeStruct((n, w), BF16),
        scratch_shapes=[pltpu.VMEM((2 * Q_TILE, Q_TILE), F32)],
        compiler_params=_cparams("arbitrary", "arbitrary"),
        name="diff_prompt",
    )(q, kt, v, *lams, gsub)


def _diff_sample_kernel(q_ref, kc_ref, vc_ref, kn_ref, vn_ref, lq1_ref, lk1_ref, lq2_ref,
                        lk2_ref, gsub_ref, o_ref, *, lam_init):
    lam = _lam(lq1_ref, lk1_ref, lq2_ref, lk2_ref, lam_init)
    lo, hi, _ = _pair_masks()
    past = kc_ref.shape[0] // HB
    t = q_ref.shape[1]

    def scores(hd):
        ls = slice(hd * EB, (hd + 1) * EB)
        q = q_ref[0, :, ls]
        q12 = jnp.concatenate([q * lo, q * hi], axis=0)
        kc = kc_ref[pl.ds(hd, past, stride=HB), :].astype(BF16)
        return _dot_nt(q12, kc), _dot_nt(q12, kn_ref[0, :, ls])

    def finish(hd, sc, sn):
        ls = slice(hd * EB, (hd + 1) * EB)
        vc = vc_ref[pl.ds(hd, past, stride=HB), :].astype(BF16)
        o12 = _softmax_pv([sc, sn], [lambda p: _dot(p, vc), lambda p: _dot(p, vn_ref[0, :, ls])])
        o = o12[:t] - lam * o12[t:]
        o_ref[0, :, ls] = (_rms(o, gsub_ref[...]) * (1.0 - lam_init)).astype(o_ref.dtype)

    pending = [scores(hd) for hd in range(B_LOOKAHEAD)]
    for hd in range(HB):
        if hd + B_LOOKAHEAD < HB:
            pending.append(scores(hd + B_LOOKAHEAD))
        finish(hd, *pending.pop(0))


def _diff_sample(q, kn, vn, cache_k, cache_v, layer, lams, gsub, lam_init):
    b, t, w = q.shape
    rows = cache_k.shape[2]
    new = pl.BlockSpec((1, t, w), lambda i: (i, 0, 0))
    old = pl.BlockSpec((None, None, rows, EB), lambda i: (layer, i, 0, 0))
    vec = _resident((1, DB))
    return pl.pallas_call(
        functools.partial(_diff_sample_kernel, lam_init=lam_init),
        grid=(b,),
        in_specs=[new, old, old, new, new, vec, vec, vec, vec, _resident((1, EB))],
        out_specs=new,
        out_shape=jax.ShapeDtypeStruct((b, t, w), BF16),
        compiler_params=_cparams("parallel"),
        name="diff_sample",
    )(q, cache_k, cache_v, kn, vn, *lams, gsub)


def _rope_tables(pos):
    half = DB // 2
    inv = ROPE_THETA ** (-jnp.arange(half, dtype=F32) * 2.0 / DB)
    ang = pos.astype(F32)[:, None] * inv[None, :]
    cos, sin = jnp.cos(ang), jnp.sin(ang)
    cos2 = jnp.concatenate([cos, cos], axis=-1)
    sin2 = jnp.concatenate([-sin, sin], axis=-1)
    return jnp.tile(cos2, (1, LANES // DB)), jnp.tile(sin2, (1, LANES // DB))


def _rel_rows(table):
    table = table.astype(F32)
    n_const = REL_SPAN - (2 * MAX_REL - 1)
    const = jnp.broadcast_to(table[:, 2 * MAX_REL:], (HA, n_const))
    desc = jnp.concatenate([table[:, 1:2 * MAX_REL], const], axis=1)
    asc = jnp.concatenate([const, table[:, 1:2 * MAX_REL][:, ::-1]], axis=1)
    return asc, desc


def kernel(x_prompt, x_sample, cache_a_k, cache_a_v, cache_b_k, cache_b_v,
           g_ffn1, w1_gate, w1_up, w1_down, g_mix, w_in, g_qa, g_ka, g_qb, g_kb,
           rel_bias, lam_q1, lam_k1, lam_q2, lam_k2, g_sub, w_out,
           g_ffn2, w2_gate, w2_up, w2_down):
    bp, s, d = x_prompt.shape
    bd, t, _ = x_sample.shape
    depth = w_in.shape[0]
    past = cache_b_k.shape[2]
    a_len = cache_a_k.shape[2]
    assert a_len == BAND_PREV and t == CHUNK and s % Q_TILE == 0 and MAX_REL == LANES
    keep_p = min(BAND_PREV, s)

    cos_p, sin_p = _rope_tables(jnp.arange(s))
    reps = TOKEN_TILE // t
    cos_s, sin_s = (jnp.tile(a, (reps, 1)) for a in _rope_tables(past + jnp.arange(t)))

    cakt = jnp.transpose(cache_a_k, (0, 1, 3, 4, 2))
    cavt = jnp.transpose(cache_a_v, (0, 1, 3, 4, 2))
    cbk = cache_b_k.reshape(depth, bd, past * HB, EB)
    cbv = cache_b_v.reshape(depth, bd, past * HB, EB)

    yp = x_prompt.reshape(bp * s, d)
    ys = x_sample.reshape(bd * t, d)
    pb = sb = None
    w1 = (w1_gate.astype(BF16), w1_up.astype(BF16), w1_down.astype(BF16))
    w2 = (w2_gate.astype(BF16), w2_up.astype(BF16), w2_down.astype(BF16))
    wi = w_in.astype(BF16)
    wo = w_out.astype(BF16)
    for l in range(depth):
        lam_init = 0.8 - 0.6 * math.exp(-0.3 * l)
        row = lambda a: a[l].astype(F32).reshape(1, -1)
        tiled = lambda a: jnp.tile(row(a), (1, GROUP_W // a.shape[1]))
        gains = (tiled(g_qa), tiled(g_ka), tiled(g_qb), tiled(g_kb))
        lams = (row(lam_q1), row(lam_k1), row(lam_q2), row(lam_k2))
        gsub = row(g_sub)
        rel_asc, rel_desc = _rel_rows(rel_bias[l])

        yp = _ffn(yp, row(g_ffn1), *w1, l)
        qat, ka, vat, qb, kbt, vb, *pb = _proj(
            yp, row(g_mix), wi, *gains, cos_p, sin_p, s, l, depth, True, pb)
        oa = _band_prompt(qat, ka, vat, rel_desc, bp)
        ob = _diff_prompt(qb, kbt, vb, lams, gsub, lam_init, bp)
        yp = _out_ffn(yp, oa, ob, wo, row(g_ffn2), *w2, l)

        ys = _ffn(ys, row(g_ffn1), *w1, l)
        qa, ka, va, qb, kb, vb, *sb = _proj(
            ys, row(g_mix), wi, *gains, cos_s, sin_s, t, l, depth, False, sb)
        sh = lambda a: a.reshape(bd, t, GROUP_W)
        oa = _band_sample(sh(qa), sh(ka), sh(va), cakt, cavt, l, rel_asc, 4)
        ob = _diff_sample(sh(qb), sh(kb), sh(vb), cbk, cbv, l, lams, gsub, lam_init)
        ys = _out_ffn(ys, oa.reshape(bd * t, GROUP_W), ob.reshape(bd * t, GROUP_W),
                      wo, row(g_ffn2), *w2, l)

    def kept_prompt(a):
        return jnp.transpose(a.reshape(depth, bp, HA, DA, keep_p), (0, 1, 4, 2, 3))

    return (yp.reshape(bp, s, d), ys.reshape(bd, t, d),
            kept_prompt(pb[0]), kept_prompt(pb[1]),
            pb[2].reshape(depth, bp, s, HB, EB), pb[3].reshape(depth, bp, s, HB, EB),
            sb[0].reshape(depth, bd, t, HA, DA), sb[1].reshape(depth, bd, t, HA, DA),
            sb[2].reshape(depth, bd, t, HB, EB), sb[3].reshape(depth, bd, t, HB, EB))
```
